```python
import math
import jax, jax.numpy as jnp
from jax import lax
import numpy as np

D_MODEL = 1024
BATCH = 8
SEQ = 4096
DEPTH = 4

MIX_WIDTH = D_MODEL
CONV_CH = MIX_WIDTH // 2
CONV_K = 31
ATT_HEADS = 4
ATT_HD = 64
ATT_VD = 2 * ATT_HD
ATT_OUT = ATT_HEADS * ATT_VD
QK_COLS = ATT_HEADS * 2 * ATT_HD
IN_COLS = 2 * CONV_CH + 2 * QK_COLS + ATT_OUT
ROT_DIM = ATT_HD // 4
ROPE_THETA = 500000.0
D_FF = 2816
FFN_K = 3
Q_BLOCK = 128
EPS = 1e-6

kernel_name = "hybrid_conformer_diffattn_convffn"


def rms_norm(x, g):
    xf = x.astype(jnp.float32)
    y = xf * lax.rsqrt(jnp.mean(xf * xf, axis=-1, keepdims=True) + EPS)
    return (y * g.astype(jnp.float32)).astype(x.dtype)


def layer_norm(x, g, b):
    xf = x.astype(jnp.float32)
    mu = jnp.mean(xf, axis=-1, keepdims=True)
    var = jnp.mean(jnp.square(xf - mu), axis=-1, keepdims=True)
    y = (xf - mu) * lax.rsqrt(var + EPS)
    return (y * g.astype(jnp.float32) + b.astype(jnp.float32)).astype(x.dtype)


def causal_dwconv(x, w, b):
    k, c = w.shape
    y = lax.conv_general_dilated(
        x, w[:, None, :].astype(x.dtype), window_strides=(1,), padding=[(k - 1, 0)],
        dimension_numbers=("NWC", "WIO", "NWC"), feature_group_count=c)
    return y + b.astype(x.dtype)


def rope_tables(seq):
    pos = jnp.arange(seq, dtype=jnp.float32)
    inv_freq = ROPE_THETA ** (-jnp.arange(0, ROT_DIM, 2, dtype=jnp.float32) / ROT_DIM)
    ang = pos[:, None] * inv_freq[None, :]
    return jnp.cos(ang), jnp.sin(ang)


def apply_partial_rope(t, cos, sin):
    c = cos[None, :, None, None, :].astype(t.dtype)
    s = sin[None, :, None, None, :].astype(t.dtype)
    half = ROT_DIM // 2
    t1, t2, rest = t[..., :half], t[..., half:ROT_DIM], t[..., ROT_DIM:]
    return jnp.concatenate([t1 * c - t2 * s, t2 * c + t1 * s, rest], axis=-1)


def diff_attention(q1, q2, k1, k2, v, lam):
    b, h, s, d = q1.shape
    nb = s // Q_BLOCK
    scale = 1.0 / math.sqrt(d)

    def to_blocks(t):
        return t.reshape(b, h, nb, Q_BLOCK, d).transpose(2, 0, 1, 3, 4)

    kpos = jnp.arange(s)

    def one_block(args):
        q1b, q2b, start = args
        qpos = start + jnp.arange(Q_BLOCK)
        causal = kpos[None, :] <= qpos[:, None]
        s1 = jnp.einsum("bhqd,bhkd->bhqk", q1b, k1).astype(jnp.float32) * scale
        s2 = jnp.einsum("bhqd,bhkd->bhqk", q2b, k2).astype(jnp.float32) * scale
        neg = jnp.finfo(jnp.float32).min
        p1 = jax.nn.softmax(jnp.where(causal, s1, neg), axis=-1)
        p2 = jax.nn.softmax(jnp.where(causal, s2, neg), axis=-1)
        a = (p1 - lam * p2).astype(v.dtype)
        return jnp.einsum("bhqk,bhkv->bhqv", a, v)

    starts = jnp.arange(nb, dtype=jnp.int32) * Q_BLOCK
    out = lax.map(one_block, (to_blocks(q1), to_blocks(q2), starts))
    return out.transpose(1, 2, 0, 3, 4).reshape(b, h, s, v.shape[-1])


def setup_inputs(seed: int = 0) -> dict:
    key = jax.random.key(seed)
    ks = jax.random.split(key, 24)
    f32 = jnp.float32

    def nrm(k, shape, scale):
        return jax.random.normal(k, shape, f32) * scale

    def gain(k, shape):
        return 1.0 + 0.02 * jax.random.normal(k, shape, f32)

    return {
        "x": jax.random.normal(ks[0], (BATCH, SEQ, D_MODEL), f32),
        "pre_mix_norm": gain(ks[1], (DEPTH, D_MODEL)),
        "w_in": nrm(ks[2], (DEPTH, D_MODEL, IN_COLS), D_MODEL ** -0.5),
        "conv_w": nrm(ks[3], (DEPTH, CONV_K, CONV_CH), CONV_K ** -0.5),
        "conv_b": nrm(ks[4], (DEPTH, CONV_CH), 0.02),
        "conv_ln_g": gain(ks[5], (DEPTH, CONV_CH)),
        "conv_ln_b": nrm(ks[6], (DEPTH, CONV_CH), 0.02),
        "lambda_q1": nrm(ks[7], (DEPTH, ATT_HD), 0.1),
        "lambda_k1": nrm(ks[8], (DEPTH, ATT_HD), 0.1),
        "lambda_q2": nrm(ks[9], (DEPTH, ATT_HD), 0.1),
        "lambda_k2": nrm(ks[10], (DEPTH, ATT_HD), 0.1),
        "subln_g": gain(ks[11], (DEPTH, ATT_VD)),
        "w_out": nrm(ks[12], (DEPTH, CONV_CH + ATT_OUT, D_MODEL), (CONV_CH + ATT_OUT) ** -0.5),
        "post_mix_norm": gain(ks[13], (DEPTH, D_MODEL)),
        "pre_ffn_norm": gain(ks[14], (DEPTH, D_MODEL)),
        "w_up": nrm(ks[15], (DEPTH, D_MODEL, 2 * D_FF), D_MODEL ** -0.5),
        "ffn_conv_w": nrm(ks[16], (DEPTH, FFN_K, 2 * D_FF), FFN_K ** -0.5),
        "ffn_conv_b": nrm(ks[17], (DEPTH, 2 * D_FF), 0.02),
        "w_down": nrm(ks[18], (DEPTH, D_FF, D_MODEL), D_FF ** -0.5),
        "post_ffn_norm": gain(ks[19], (DEPTH, D_MODEL)),
    }


def reference(x, pre_mix_norm, w_in, conv_w, conv_b, conv_ln_g, conv_ln_b,
              lambda_q1, lambda_k1, lambda_q2, lambda_k2, subln_g, w_out,
              post_mix_norm, pre_ffn_norm, w_up, ffn_conv_w, ffn_conv_b, w_down,
              post_ffn_norm):
    b, s, _ = x.shape
    cos, sin = rope_tables(s)
    for l in range(DEPTH):
        lam_init = 0.8 - 0.6 * math.exp(-0.3 * l)
        h = rms_norm(x, pre_mix_norm[l])
        z = jnp.einsum("bsd,dc->bsc", h, w_in[l])
        c_a, c_g, zq, zk, zv = jnp.split(
            z, np.cumsum([CONV_CH, CONV_CH, QK_COLS, QK_COLS]).tolist(), axis=-1)
        u = c_a * jax.nn.sigmoid(c_g)
        u = causal_dwconv(u, conv_w[l], conv_b[l])
        u = jax.nn.silu(layer_norm(u, conv_ln_g[l], conv_ln_b[l]))
        q = apply_partial_rope(zq.reshape(b, s, ATT_HEADS, 2, ATT_HD), cos, sin)
        k = apply_partial_rope(zk.reshape(b, s, ATT_HEADS, 2, ATT_HD), cos, sin)
        v = zv.reshape(b, s, ATT_HEADS, ATT_VD).transpose(0, 2, 1, 3)
        q1, q2 = q[..., 0, :].transpose(0, 2, 1, 3), q[..., 1, :].transpose(0, 2, 1, 3)
        k1, k2 = k[..., 0, :].transpose(0, 2, 1, 3), k[..., 1, :].transpose(0, 2, 1, 3)
        lam = (jnp.exp(jnp.sum(lambda_q1[l].astype(jnp.float32) * lambda_k1[l].astype(jnp.float32)))
               - jnp.exp(jnp.sum(lambda_q2[l].astype(jnp.float32) * lambda_k2[l].astype(jnp.float32)))
               + lam_init)
        o = diff_attention(q1, q2, k1, k2, v, lam)
        o = rms_norm(o, subln_g[l]) * (1.0 - lam_init)
        o = o.transpose(0, 2, 1, 3).reshape(b, s, ATT_OUT)
        m = jnp.einsum("bsc,cd->bsd", jnp.concatenate([u, o], axis=-1), w_out[l])
        x = x + rms_norm(m, post_mix_norm[l])
        h = rms_norm(x, pre_ffn_norm[l])
        up = jnp.einsum("bsd,df->bsf", h, w_up[l])
        up = causal_dwconv(up, ffn_conv_w[l], ffn_conv_b[l])
        g, val = jnp.split(up, 2, axis=-1)
        f = jnp.einsum("bsf,fd->bsd", jax.nn.gelu(g, approximate=True) * val, w_down[l])
        x = x + rms_norm(f, post_ffn_norm[l])
    return x
```

```python
import functools
import math

import jax
import jax.numpy as jnp
from jax import lax
from jax.experimental import pallas as pl
from jax.experimental.pallas import tpu as pltpu

D_MODEL = 1024
DEPTH = 4
CONV_CH = 512
CONV_K = 31
ATT_HEADS = 4
ATT_HD = 64
ATT_VD = 2 * ATT_HD
ATT_OUT = ATT_HEADS * ATT_VD
QK_COLS = ATT_HEADS * 2 * ATT_HD
IN_COLS = 2 * CONV_CH + 2 * QK_COLS + ATT_OUT
ROT_DIM = ATT_HD // 4
ROPE_THETA = 500000.0
D_FF = 2816
FFN_K = 3
EPS = 1e-6

LANES = 128
VMEM_LIMIT = 48 * 1024 * 1024

TM_PROJ = 512
TS_CONV = 256
CONV_HALO = 32
CONV_ROWS = 32
TQ = 512
TM_FFN = 512
FFN_HALO = 16
FC = 256
NEG_BIG = -1e30

F32 = jnp.float32
BF16 = jnp.bfloat16


def _params(*sem):
    return pltpu.CompilerParams(dimension_semantics=sem, vmem_limit_bytes=VMEM_LIMIT)


def _rms(x, g):
    return x * lax.rsqrt(jnp.mean(x * x, axis=-1, keepdims=True) + EPS) * g


def _in_proj_kernel(x_ref, g_ref, w_ref, rc_ref, ra_ref, rb_ref,
                    u_ref, q_ref, k_ref, v_ref):
    h = _rms(x_ref[...], g_ref[...]).astype(BF16)

    def proj(c0):
        return jnp.dot(h, w_ref[:, c0:c0 + CONV_CH], preferred_element_type=F32)

    za = proj(0)
    zg = proj(CONV_CH)
    u_ref[...] = za * jax.nn.sigmoid(zg)

    rc, ra, rb = rc_ref[...], ra_ref[...], rb_ref[...]

    def rope(z):
        return (z * rc + pltpu.roll(z, ROT_DIM // 2, 1) * ra
                + pltpu.roll(z, LANES - ROT_DIM // 2, 1) * rb)

    zq = proj(2 * CONV_CH)
    zk = proj(2 * CONV_CH + QK_COLS)
    scale = 1.0 / math.sqrt(ATT_HD)
    for hd in range(ATT_HEADS):
        sl = slice(hd * LANES, (hd + 1) * LANES)
        q_ref[:, sl] = (rope(zq[:, sl]) * scale).astype(BF16)
        k_ref[:, sl] = rope(zk[:, sl]).astype(BF16)
    v_ref[...] = proj(2 * CONV_CH + 2 * QK_COLS).astype(BF16)


def _in_proj(x, g, w, rc, ra, rb, seq):
    n = x.shape[0]
    tm = TM_PROJ
    tiles_per_seq = seq // tm
    row = lambda i: (i, 0)
    fixed = lambda i: (0, 0)
    pos = lambda i: (i % tiles_per_seq, 0)
    return pl.pallas_call(
        _in_proj_kernel,
        grid=(n // tm,),
        in_specs=[
            pl.BlockSpec((tm, D_MODEL), row),
            pl.BlockSpec((1, D_MODEL), fixed),
            pl.BlockSpec((D_MODEL, IN_COLS), fixed),
            pl.BlockSpec((tm, LANES), pos),
            pl.BlockSpec((tm, LANES), pos),
            pl.BlockSpec((tm, LANES), pos),
        ],
        out_specs=[
            pl.BlockSpec((tm, CONV_CH), row),
            pl.BlockSpec((tm, QK_COLS), row),
            pl.BlockSpec((tm, QK_COLS), row),
            pl.BlockSpec((tm, ATT_OUT), row),
        ],
        out_shape=[
            jax.ShapeDtypeStruct((n, CONV_CH), F32),
            jax.ShapeDtypeStruct((n, QK_COLS), BF16),
            jax.ShapeDtypeStruct((n, QK_COLS), BF16),
            jax.ShapeDtypeStruct((n, ATT_OUT), BF16),
        ],
        compiler_params=_params("parallel"),
        name="in_proj",
    )(x, g, w, rc, ra, rb)


def _conv_kernel(tiles_per_seq, ucur_ref, uprev_ref, w_ref, b_ref, g_ref, beta_ref,
                 o_ref, buf_ref):
    first = (pl.program_id(0) % tiles_per_seq) == 0
    buf_ref[0:CONV_HALO, :] = jnp.where(first, 0.0, uprev_ref[...])
    buf_ref[CONV_HALO:, :] = ucur_ref[...]
    ts = ucur_ref.shape[0]
    base = CONV_HALO - (CONV_K - 1)
    for r0 in range(0, ts, CONV_ROWS):
        acc = jnp.zeros((CONV_ROWS, CONV_CH), F32)
        for k in range(CONV_K):
            acc = acc + w_ref[k:k + 1, :] * buf_ref[pl.ds(r0 + base + k, CONV_ROWS), :]
        acc = acc + b_ref[...]
        mu = jnp.mean(acc, axis=-1, keepdims=True)
        d = acc - mu
        var = jnp.mean(d * d, axis=-1, keepdims=True)
        y = d * lax.rsqrt(var + EPS) * g_ref[...] + beta_ref[...]
        o_ref[r0:r0 + CONV_ROWS, :] = (y * jax.nn.sigmoid(y)).astype(BF16)


def _conv(u, w, b, g, beta, seq):
    n = u.shape[0]
    ts = TS_CONV
    tiles_per_seq = seq // ts
    halo_per_tile = ts // CONV_HALO
    row = lambda i: (i, 0)
    fixed = lambda i: (0, 0)
    prev = lambda i: (jnp.maximum(i * halo_per_tile - 1, 0), 0)
    return pl.pallas_call(
        functools.partial(_conv_kernel, tiles_per_seq),
        grid=(n // ts,),
        in_specs=[
            pl.BlockSpec((ts, CONV_CH), row),
            pl.BlockSpec((CONV_HALO, CONV_CH), prev),
            pl.BlockSpec((CONV_K, CONV_CH), fixed),
            pl.BlockSpec((1, CONV_CH), fixed),
            pl.BlockSpec((1, CONV_CH), fixed),
            pl.BlockSpec((1, CONV_CH), fixed),
        ],
        out_specs=pl.BlockSpec((ts, CONV_CH), row),
        out_shape=jax.ShapeDtypeStruct((n, CONV_CH), BF16),
        scratch_shapes=[pltpu.VMEM((CONV_HALO + ts, CONV_CH), F32)],
        compiler_params=_params("parallel"),
        name="conv_ln_swish",
    )(u, u, w, b, g, beta)


def _attn_kernel(q_ref, k_ref, v_ref, lq1_ref, lk1_ref, lq2_ref, lk2_ref, lamc_ref, sg_ref,
                 o_ref, m_ref, l_ref, acc_ref):
    qi = pl.program_id(2)
    tq = q_ref.shape[1]
    q = q_ref[0]
    lane = lax.broadcasted_iota(jnp.int32, q.shape, 1)
    zero = jnp.zeros_like(q)
    qs = (jnp.where(lane < ATT_HD, q, zero), jnp.where(lane >= ATT_HD, q, zero))

    m_ref[...] = jnp.full(m_ref.shape, NEG_BIG, F32)
    l_ref[...] = jnp.zeros(l_ref.shape, F32)
    acc_ref[...] = jnp.zeros(acc_ref.shape, F32)

    def step(j, masked):
        kb = k_ref[0, pl.ds(j * tq, tq), :]
        vb = v_ref[0, pl.ds(j * tq, tq), :]
        for c in range(2):
            s = lax.dot_general(qs[c], kb, (((1,), (1,)), ((), ())),
                                preferred_element_type=F32)
            if masked:
                r = lax.broadcasted_iota(jnp.int32, s.shape, 0)
                cidx = lax.broadcasted_iota(jnp.int32, s.shape, 1)
                s = jnp.where(cidx <= r, s, NEG_BIG)
            m_prev = m_ref[c]
            m_new = jnp.maximum(m_prev, jnp.max(s, axis=-1, keepdims=True))
            alpha = jnp.exp(m_prev - m_new)
            p = jnp.exp(s - m_new)
            l_ref[c] = alpha * l_ref[c] + jnp.sum(p, axis=-1, keepdims=True)
            acc_ref[c] = alpha * acc_ref[c] + jnp.dot(p.astype(BF16), vb,
                                                      preferred_element_type=F32)
            m_ref[c] = m_new

    def body(j, carry):
        step(j, False)
        return carry

    lax.fori_loop(0, qi, body, 0)
    step(qi, True)

    lam_init = lamc_ref[0:1, 0:1]
    out_scale = lamc_ref[0:1, 1:2]
    lam = (jnp.exp(jnp.sum(lq1_ref[...] * lk1_ref[...], axis=-1, keepdims=True))
           - jnp.exp(jnp.sum(lq2_ref[...] * lk2_ref[...], axis=-1, keepdims=True))
           + lam_init)
    o = acc_ref[0] / l_ref[0] - lam * (acc_ref[1] / l_ref[1])
    o_ref[0] = (_rms(o, sg_ref[...]) * out_scale).astype(BF16)


def _attn(q, k, v, lq1, lk1, lq2, lk2, lamc, sg):
    b, s, _ = q.shape
    tq = TQ
    head_q = lambda bi, hi, qi: (bi, qi, hi)
    head_kv = lambda bi, hi, qi: (bi, 0, hi)
    fixed = lambda bi, hi, qi: (0, 0)
    vec = pl.BlockSpec((1, ATT_HD), fixed)
    return pl.pallas_call(
        _attn_kernel,
        grid=(b, ATT_HEADS, s // tq),
        in_specs=[
            pl.BlockSpec((1, tq, LANES), head_q),
            pl.BlockSpec((1, s, LANES), head_kv),
            pl.BlockSpec((1, s, ATT_VD), head_kv),
            vec, vec, vec, vec,
            pl.BlockSpec((1, LANES), fixed),
            pl.BlockSpec((1, ATT_VD), fixed),
        ],
        out_specs=pl.BlockSpec((1, tq, ATT_VD), head_q),
        out_shape=jax.ShapeDtypeStruct((b, s, ATT_OUT), BF16),
        scratch_shapes=[
            pltpu.VMEM((2, tq, 1), F32),
            pltpu.VMEM((2, tq, 1), F32),
            pltpu.VMEM((2, tq, ATT_VD), F32),
        ],
        compiler_params=_params("parallel", "parallel", "parallel"),
        name="diff_attn",
    )(q, k, v, lq1, lk1, lq2, lk2, lamc, sg)


def _out_proj_kernel(x_ref, u_ref, o_ref, w_ref, g_ref, y_ref):
    m = jnp.dot(u_ref[...], w_ref[0:CONV_CH, :], preferred_element_type=F32)
    m = m + jnp.dot(o_ref[...], w_ref[CONV_CH:, :], preferred_element_type=F32)
    y_ref[...] = x_ref[...] + _rms(m, g_ref[...])


def _out_proj(x, u, o, w, g):
    n = x.shape[0]
    tm = TM_PROJ
    row = lambda i: (i, 0)
    fixed = lambda i: (0, 0)
    return pl.pallas_call(
        _out_proj_kernel,
        grid=(n // tm,),
        in_specs=[
            pl.BlockSpec((tm, D_MODEL), row),
            pl.BlockSpec((tm, CONV_CH), row),
            pl.BlockSpec((tm, ATT_OUT), row),
            pl.BlockSpec((CONV_CH + ATT_OUT, D_MODEL), fixed),
            pl.BlockSpec((1, D_MODEL), fixed),
        ],
        out_specs=pl.BlockSpec((tm, D_MODEL), row),
        out_shape=jax.ShapeDtypeStruct((n, D_MODEL), F32),
        compiler_params=_params("parallel"),
        name="out_proj",
    )(x, u, o, w, g)


def _ffn_kernel(tiles_per_seq, x_ref, xprev_ref, gpre_ref, wg_ref, wv_ref, cwg_ref, cwv_ref,
                cbg_ref, cbv_ref, wd_ref, gpost_ref, y_ref, h_ref, acc_ref):
    i = pl.program_id(0)
    j = pl.program_id(1)
    tm = x_ref.shape[0]

    @pl.when(j == 0)
    def _():
        first = (i % tiles_per_seq) == 0
        hp = _rms(xprev_ref[...], gpre_ref[...])
        h_ref[0:FFN_HALO, :] = jnp.where(first, 0.0, hp).astype(BF16)
        h_ref[FFN_HALO:, :] = _rms(x_ref[...], gpre_ref[...]).astype(BF16)
        acc_ref[...] = jnp.zeros(acc_ref.shape, F32)

    h = h_ref[...]

    def conv_up(w_ref, cw_ref, cb_ref):
        up = jnp.dot(h, w_ref[...], preferred_element_type=F32)
        out = cb_ref[...]
        for t in range(FFN_K):
            off = FFN_HALO - (FFN_K - 1) + t
            out = out + cw_ref[t:t + 1, :] * up[off:off + tm, :]
        return out

    g = conv_up(wg_ref, cwg_ref, cbg_ref)
    val = conv_up(wv_ref, cwv_ref, cbv_ref)
    act = (jax.nn.gelu(g, approximate=True) * val).astype(BF16)
    acc_ref[...] += jnp.dot(act, wd_ref[...], preferred_element_type=F32)

    @pl.when(j == pl.num_programs(1) - 1)
    def _():
        y_ref[...] = x_ref[...] + _rms(acc_ref[...], gpost_ref[...])


def _ffn(x, gpre, wup, cw, cb, wd, gpost, seq):
    n = x.shape[0]
    tm = TM_FFN
    nf = D_FF // FC
    tiles_per_seq = seq // tm
    halo_per_tile = tm // FFN_HALO
    row = lambda i, j: (i, 0)
    fixed = lambda i, j: (0, 0)
    prev = lambda i, j: (jnp.maximum(i * halo_per_tile - 1, 0), 0)
    gcol = lambda i, j: (0, j)
    vcol = lambda i, j: (0, nf + j)
    return pl.pallas_call(
        functools.partial(_ffn_kernel, tiles_per_seq),
        grid=(n // tm, nf),
        in_specs=[
            pl.BlockSpec((tm, D_MODEL), row),
            pl.BlockSpec((FFN_HALO, D_MODEL), prev),
            pl.BlockSpec((1, D_MODEL), fixed),
            pl.BlockSpec((D_MODEL, FC), gcol),
            pl.BlockSpec((D_MODEL, FC), vcol),
            pl.BlockSpec((FFN_K, FC), gcol),
            pl.BlockSpec((FFN_K, FC), vcol),
            pl.BlockSpec((1, FC), gcol),
            pl.BlockSpec((1, FC), vcol),
            pl.BlockSpec((FC, D_MODEL), lambda i, j: (j, 0)),
            pl.BlockSpec((1, D_MODEL), fixed),
        ],
        out_specs=pl.BlockSpec((tm, D_MODEL), row),
        out_shape=jax.ShapeDtypeStruct((n, D_MODEL), F32),
        scratch_shapes=[
            pltpu.VMEM((FFN_HALO + tm, D_MODEL), BF16),
            pltpu.VMEM((tm, D_MODEL), F32),
        ],
        compiler_params=_params("parallel", "arbitrary"),
        name="ffn",
    )(x, x, gpre, wup, wup, cw, cw, cb, cb, wd, gpost)


def _rope_tables(seq):
    half = ROT_DIM // 2
    pos = jnp.arange(seq, dtype=F32)
    inv_freq = ROPE_THETA ** (-jnp.arange(0, ROT_DIM, 2, dtype=F32) / ROT_DIM)
    ang = pos[:, None] * inv_freq[None, :]
    cos, sin = jnp.cos(ang), jnp.sin(ang)
    ones = jnp.ones((seq, ATT_HD - ROT_DIM), F32)
    zeros_rest = jnp.zeros((seq, ATT_HD - ROT_DIM), F32)
    zeros_half = jnp.zeros((seq, half), F32)
    c = jnp.concatenate([cos, cos, ones], axis=1)
    a = jnp.concatenate([zeros_half, sin, zeros_rest], axis=1)
    b = jnp.concatenate([-sin, zeros_half, zeros_rest], axis=1)
    tile2 = lambda t: jnp.concatenate([t, t], axis=1)
    return tile2(c), tile2(a), tile2(b)


def kernel(x, pre_mix_norm, w_in, conv_w, conv_b, conv_ln_g, conv_ln_b, lambda_q1, lambda_k1,
           lambda_q2, lambda_k2, subln_g, w_out, post_mix_norm, pre_ffn_norm, w_up, ffn_conv_w,
           ffn_conv_b, w_down, post_ffn_norm):
    b, s, d = x.shape
    n = b * s
    rc, ra, rb = _rope_tables(s)
    xf = x.reshape(n, d)
    row = lambda t: t.reshape(1, -1)
    for l in range(DEPTH):
        lam_init = 0.8 - 0.6 * math.exp(-0.3 * l)
        lamc = jnp.zeros((1, LANES), F32).at[0, 0].set(lam_init).at[0, 1].set(1.0 - lam_init)
        u, q, k, v = _in_proj(xf, row(pre_mix_norm[l]), w_in[l].astype(BF16), rc, ra, rb, s)
        u = _conv(u, conv_w[l], row(conv_b[l]), row(conv_ln_g[l]), row(conv_ln_b[l]), s)
        o = _attn(q.reshape(b, s, QK_COLS), k.reshape(b, s, QK_COLS), v.reshape(b, s, ATT_OUT),
                  row(lambda_q1[l]), row(lambda_k1[l]), row(lambda_q2[l]), row(lambda_k2[l]),
                  lamc, row(subln_g[l]))
        xf = _out_proj(xf, u, o.reshape(n, ATT_OUT), w_out[l].astype(BF16),
                       row(post_mix_norm[l]))
        xf = _ffn(xf, row(pre_ffn_norm[l]), w_up[l].astype(BF16), ffn_conv_w[l],
                  row(ffn_conv_b[l]), w_down[l].astype(BF16), row(post_ffn_norm[l]), s)
    return xf.reshape(b, s, d)
```

```python
import functools
import math

import jax
import jax.numpy as jnp
from jax import lax
from jax.experimental import pallas as pl
from jax.experimental.pallas import tpu as pltpu

D_MODEL = 1024
DEPTH = 4
CONV_CH = 512
CONV_K = 31
ATT_HEADS = 4
ATT_HD = 64
ATT_VD = 2 * ATT_HD
ATT_OUT = ATT_HEADS * ATT_VD
QK_COLS = ATT_HEADS * 2 * ATT_HD
IN_COLS = 2 * CONV_CH + 2 * QK_COLS + ATT_OUT
ROT_DIM = ATT_HD // 4
ROPE_THETA = 500000.0
D_FF = 2816
FFN_K = 3
EPS = 1e-6

LANES = 128
SUBLANES = 8
VMEM_LIMIT = 56 * 1024 * 1024

TM_PROJ = 512
TS_CONV = 256
CONV_HALO = 32
CONV_ROWS = 32
TQ = 512
TM_FFN = 512
FFN_HALO = 16
FC = 256
NEG_BIG = -1e30

F32 = jnp.float32
BF16 = jnp.bfloat16


def _params(*sem):
    return pltpu.CompilerParams(dimension_semantics=sem, vmem_limit_bytes=VMEM_LIMIT)


def _rms(x, g):
    return x * lax.rsqrt(jnp.mean(x * x, axis=-1, keepdims=True) + EPS) * g


def _in_proj_kernel(x_ref, g_ref, w_ref, rc_ref, ra_ref, rb_ref,
                    u_ref, q_ref, k_ref, vt_ref):
    h = _rms(x_ref[...], g_ref[...]).astype(BF16)

    def proj(c0):
        return jnp.dot(h, w_ref[:, c0:c0 + CONV_CH], preferred_element_type=F32)

    za = proj(0)
    zg = proj(CONV_CH)
    u_ref[...] = za * jax.nn.sigmoid(zg)

    rc, ra, rb = rc_ref[...], ra_ref[...], rb_ref[...]

    def rope(z):
        return (z * rc + pltpu.roll(z, ROT_DIM // 2, 1) * ra
                + pltpu.roll(z, LANES - ROT_DIM // 2, 1) * rb)

    zq = proj(2 * CONV_CH)
    zk = proj(2 * CONV_CH + QK_COLS)
    scale = math.log2(math.e) / math.sqrt(ATT_HD)
    for hd in range(ATT_HEADS):
        sl = slice(hd * LANES, (hd + 1) * LANES)
        q_ref[:, sl] = (rope(zq[:, sl]) * scale).astype(BF16)
        k_ref[:, sl] = rope(zk[:, sl]).astype(BF16)
    vt_ref[0] = proj(2 * CONV_CH + 2 * QK_COLS).T.astype(BF16)


def _in_proj(x, g, w, rc, ra, rb, seq):
    n = x.shape[0]
    tm = TM_PROJ
    tiles_per_seq = seq // tm
    row = lambda i: (i, 0)
    fixed = lambda i: (0, 0)
    pos = lambda i: (i % tiles_per_seq, 0)
    return pl.pallas_call(
        _in_proj_kernel,
        grid=(n // tm,),
        in_specs=[
            pl.BlockSpec((tm, D_MODEL), row),
            pl.BlockSpec((1, D_MODEL), fixed),
            pl.BlockSpec((D_MODEL, IN_COLS), fixed),
            pl.BlockSpec((tm, LANES), pos),
            pl.BlockSpec((tm, LANES), pos),
            pl.BlockSpec((tm, LANES), pos),
        ],
        out_specs=[
            pl.BlockSpec((tm, CONV_CH), row),
            pl.BlockSpec((tm, QK_COLS), row),
            pl.BlockSpec((tm, QK_COLS), row),
            pl.BlockSpec((1, ATT_OUT, tm), lambda i: (i // tiles_per_seq, 0, i % tiles_per_seq)),
        ],
        out_shape=[
            jax.ShapeDtypeStruct((n, CONV_CH), F32),
            jax.ShapeDtypeStruct((n, QK_COLS), BF16),
            jax.ShapeDtypeStruct((n, QK_COLS), BF16),
            jax.ShapeDtypeStruct((n // seq, ATT_OUT, seq), BF16),
        ],
        compiler_params=_params("parallel"),
        name="in_proj",
    )(x, g, w, rc, ra, rb)


def _conv_kernel(tiles_per_seq, ucur_ref, uprev_ref, w_ref, b_ref, g_ref, beta_ref,
                 o_ref, sh_ref):
    first = (pl.program_id(0) % tiles_per_seq) == 0
    ts = ucur_ref.shape[0]
    sh_ref[0, 0:CONV_HALO, :] = jnp.where(first, 0.0, uprev_ref[...])
    sh_ref[0, CONV_HALO:, :] = ucur_ref[...]
    win = sh_ref[0]
    nwin = CONV_HALO + ts
    for r in range(1, SUBLANES):
        sh_ref[r] = pltpu.roll(win, nwin - r, 0)
    base = CONV_HALO - (CONV_K - 1)
    groups = CONV_ROWS // SUBLANES
    for r0 in range(0, ts, CONV_ROWS):
        acc = jnp.zeros((groups, SUBLANES, CONV_CH), F32)
        for k in range(CONV_K):
            phase = (base + k) % SUBLANES
            start = r0 + base + k - phase
            slab = sh_ref[phase, start:start + CONV_ROWS, :]
            acc = acc + w_ref[k] * slab.reshape(groups, SUBLANES, CONV_CH)
        acc = acc.reshape(CONV_ROWS, CONV_CH) + b_ref[...]
        mu = jnp.mean(acc, axis=-1, keepdims=True)
        d = acc - mu
        var = jnp.mean(d * d, axis=-1, keepdims=True)
        y = d * lax.rsqrt(var + EPS) * g_ref[...] + beta_ref[...]
        o_ref[r0:r0 + CONV_ROWS, :] = (y * jax.nn.sigmoid(y)).astype(BF16)


def _conv(u, w, b, g, beta, seq):
    n = u.shape[0]
    ts = TS_CONV
    tiles_per_seq = seq // ts
    halo_per_tile = ts // CONV_HALO
    row = lambda i: (i, 0)
    fixed = lambda i: (0, 0)
    prev = lambda i: (jnp.maximum(i * halo_per_tile - 1, 0), 0)
    w = jnp.broadcast_to(w[:, None, :], (CONV_K, SUBLANES, CONV_CH))
    return pl.pallas_call(
        functools.partial(_conv_kernel, tiles_per_seq),
        grid=(n // ts,),
        in_specs=[
            pl.BlockSpec((ts, CONV_CH), row),
            pl.BlockSpec((CONV_HALO, CONV_CH), prev),
            pl.BlockSpec((CONV_K, SUBLANES, CONV_CH), lambda i: (0, 0, 0)),
            pl.BlockSpec((1, CONV_CH), fixed),
            pl.BlockSpec((1, CONV_CH), fixed),
            pl.BlockSpec((1, CONV_CH), fixed),
        ],
        out_specs=pl.BlockSpec((ts, CONV_CH), row),
        out_shape=jax.ShapeDtypeStruct((n, CONV_CH), BF16),
        scratch_shapes=[pltpu.VMEM((SUBLANES, CONV_HALO + ts, CONV_CH), F32)],
        compiler_params=_params("parallel"),
        name="conv_ln_swish",
    )(u, u, w, b, g, beta)


def _attn_kernel(q_ref, k_ref, vt_ref, lq1_ref, lk1_ref, lq2_ref, lk2_ref, lamc_ref, sg_ref,
                 o_ref, sa_ref, sb_ref, m_ref, l_ref, acc_ref):
    qi = pl.program_id(2)
    tq = q_ref.shape[1]
    q = q_ref[0]
    lane = lax.broadcasted_iota(jnp.int32, q.shape, 1)
    zero = jnp.zeros_like(q)
    qs = (jnp.where(lane < ATT_HD, q, zero), jnp.where(lane >= ATT_HD, q, zero))

    m_ref[...] = jnp.full(m_ref.shape, NEG_BIG, F32)
    l_ref[...] = jnp.zeros(l_ref.shape, F32)
    acc_ref[...] = jnp.zeros(acc_ref.shape, F32)

    def scores(j, s_ref):
        start = pl.multiple_of(j * tq, tq)
        kb = k_ref[0, pl.ds(start, tq), :]
        for c in range(2):
            s_ref[c] = lax.dot_general(kb, qs[c], (((1,), (1,)), ((), ())),
                                       preferred_element_type=F32)

    def softmax_pv(j, s_ref, masked):
        start = pl.multiple_of(j * tq, tq)
        vtb = vt_ref[0, :, pl.ds(start, tq)]
        for c in range(2):
            st = s_ref[c]
            if masked:
                key = lax.broadcasted_iota(jnp.int32, st.shape, 0)
                qry = lax.broadcasted_iota(jnp.int32, st.shape, 1)
                st = jnp.where(key <= qry, st, NEG_BIG)
            m_prev = m_ref[c]
            m_new = jnp.maximum(m_prev, jnp.max(st, axis=0, keepdims=True))
            alpha = jnp.exp2(m_prev - m_new)
            pt = jnp.exp2(st - m_new)
            l_ref[c] = alpha * l_ref[c] + jnp.sum(pt, axis=0, keepdims=True)
            acc_ref[c] = alpha * acc_ref[c] + jnp.dot(vtb, pt.astype(BF16),
                                                      preferred_element_type=F32)
            m_ref[c] = m_new

    def by_parity(j, fn):
        lax.cond(j % 2 == 0, lambda: fn(sa_ref, sb_ref), lambda: fn(sb_ref, sa_ref))

    scores(0, sa_ref)

    def body(j, carry):
        def run(cur_ref, nxt_ref):
            scores(j + 1, nxt_ref)
            softmax_pv(j, cur_ref, False)
        by_parity(j, run)
        return carry

    lax.fori_loop(0, qi, body, 0)
    by_parity(qi, lambda cur_ref, nxt_ref: softmax_pv(qi, cur_ref, True))

    lam_init = lamc_ref[0:1, 0:1]
    out_scale = lamc_ref[0:1, 1:2]
    lam = (jnp.exp(jnp.sum(lq1_ref[...] * lk1_ref[...], axis=-1, keepdims=True))
           - jnp.exp(jnp.sum(lq2_ref[...] * lk2_ref[...], axis=-1, keepdims=True))
           + lam_init)
    ot = acc_ref[0] / l_ref[0] - lam * (acc_ref[1] / l_ref[1])
    o_ref[0] = (_rms(ot.T, sg_ref[...]) * out_scale).astype(BF16)


def _attn(q, k, vt, lq1, lk1, lq2, lk2, lamc, sg):
    b, s, _ = q.shape
    tq = TQ
    fixed = lambda bi, hi, qi: (0, 0)
    vec = pl.BlockSpec((1, ATT_HD), fixed)
    return pl.pallas_call(
        _attn_kernel,
        grid=(b, ATT_HEADS, s // tq),
        in_specs=[
            pl.BlockSpec((1, tq, LANES), lambda bi, hi, qi: (bi, qi, hi)),
            pl.BlockSpec((1, s, LANES), lambda bi, hi, qi: (bi, 0, hi)),
            pl.BlockSpec((1, ATT_VD, s), lambda bi, hi, qi: (bi, hi, 0)),
            vec, vec, vec, vec,
            pl.BlockSpec((1, LANES), fixed),
            pl.BlockSpec((1, ATT_VD), fixed),
        ],
        out_specs=pl.BlockSpec((1, tq, ATT_VD), lambda bi, hi, qi: (bi, qi, hi)),
        out_shape=jax.ShapeDtypeStruct((b, s, ATT_OUT), BF16),
        scratch_shapes=[
            pltpu.VMEM((2, tq, tq), F32),
            pltpu.VMEM((2, tq, tq), F32),
            pltpu.VMEM((2, 1, tq), F32),
            pltpu.VMEM((2, 1, tq), F32),
            pltpu.VMEM((2, ATT_VD, tq), F32),
        ],
        compiler_params=_params("parallel", "parallel", "parallel"),
        name="diff_attn",
    )(q, k, vt, lq1, lk1, lq2, lk2, lamc, sg)


def _out_proj_kernel(x_ref, u_ref, o_ref, w_ref, g_ref, y_ref):
    m = jnp.dot(u_ref[...], w_ref[0:CONV_CH, :], preferred_element_type=F32)
    m = m + jnp.dot(o_ref[...], w_ref[CONV_CH:, :], preferred_element_type=F32)
    y_ref[...] = x_ref[...] + _rms(m, g_ref[...])


def _out_proj(x, u, o, w, g):
    n = x.shape[0]
    tm = TM_PROJ
    row = lambda i: (i, 0)
    fixed = lambda i: (0, 0)
    return pl.pallas_call(
        _out_proj_kernel,
        grid=(n // tm,),
        in_specs=[
            pl.BlockSpec((tm, D_MODEL), row),
            pl.BlockSpec((tm, CONV_CH), row),
            pl.BlockSpec((tm, ATT_OUT), row),
            pl.BlockSpec((CONV_CH + ATT_OUT, D_MODEL), fixed),
            pl.BlockSpec((1, D_MODEL), fixed),
        ],
        out_specs=pl.BlockSpec((tm, D_MODEL), row),
        out_shape=jax.ShapeDtypeStruct((n, D_MODEL), F32),
        compiler_params=_params("parallel"),
        name="out_proj",
    )(x, u, o, w, g)


def _ffn_kernel(tiles_per_seq, x_ref, xprev_ref, gpre_ref, wup_ref, cw_ref, cb_ref, wd_ref,
                gpost_ref, y_ref, h_ref, ua_ref, ub_ref, acc_ref):
    tm = x_ref.shape[0]
    nf = wd_ref.shape[0]
    first = (pl.program_id(0) % tiles_per_seq) == 0
    hp = _rms(xprev_ref[...], gpre_ref[...])
    h_ref[0:FFN_HALO, :] = jnp.where(first, 0.0, hp).astype(BF16)
    h_ref[FFN_HALO:, :] = _rms(x_ref[...], gpre_ref[...]).astype(BF16)
    acc_ref[...] = jnp.zeros(acc_ref.shape, F32)

    def up_proj(j, u_ref):
        h = h_ref[...]
        u_ref[0] = jnp.dot(h, wup_ref[j], preferred_element_type=F32)
        u_ref[1] = jnp.dot(h, wup_ref[nf + j], preferred_element_type=F32)

    def process(j, u_ref):
        def conv(half, idx):
            out = cb_ref[idx]
            for t in range(FFN_K):
                off = FFN_HALO - (FFN_K - 1) + t
                out = out + cw_ref[idx, t:t + 1, :] * u_ref[half, off:off + tm, :]
            return out

        g = conv(0, j)
        val = conv(1, nf + j)
        act = (jax.nn.gelu(g, approximate=True) * val).astype(BF16)
        acc_ref[...] += jnp.dot(act, wd_ref[j], preferred_element_type=F32)

    up_proj(0, ua_ref)

    def body(p, carry):
        j = 2 * p
        up_proj(j + 1, ub_ref)
        process(j, ua_ref)
        up_proj(j + 2, ua_ref)
        process(j + 1, ub_ref)
        return carry

    lax.fori_loop(0, (nf - 1) // 2, body, 0)
    process(nf - 1, ua_ref)
    y_ref[...] = x_ref[...] + _rms(acc_ref[...], gpost_ref[...])


def _ffn(x, gpre, wup, cw, cb, wd, gpost, seq):
    n = x.shape[0]
    tm = TM_FFN
    nf = D_FF // FC
    assert nf % 2 == 1
    tiles_per_seq = seq // tm
    halo_per_tile = tm // FFN_HALO
    wup = wup.reshape(D_MODEL, 2 * nf, FC).transpose(1, 0, 2)
    cw = cw.reshape(FFN_K, 2 * nf, FC).transpose(1, 0, 2)
    cb = cb.reshape(2 * nf, 1, FC)
    wd = wd.reshape(nf, FC, D_MODEL)
    row = lambda i: (i, 0)
    prev = lambda i: (jnp.maximum(i * halo_per_tile - 1, 0), 0)

    def resident(shape):
        return pl.BlockSpec(shape, lambda i: (0,) * len(shape), pipeline_mode=pl.Buffered(1))

    return pl.pallas_call(
        functools.partial(_ffn_kernel, tiles_per_seq),
        grid=(n // tm,),
        in_specs=[
            pl.BlockSpec((tm, D_MODEL), row),
            pl.BlockSpec((FFN_HALO, D_MODEL), prev),
            resident((1, D_MODEL)),
            resident((2 * nf, D_MODEL, FC)),
            resident((2 * nf, FFN_K, FC)),
            resident((2 * nf, 1, FC)),
            resident((nf, FC, D_MODEL)),
            resident((1, D_MODEL)),
        ],
        out_specs=pl.BlockSpec((tm, D_MODEL), row),
        out_shape=jax.ShapeDtypeStruct((n, D_MODEL), F32),
        scratch_shapes=[
            pltpu.VMEM((FFN_HALO + tm, D_MODEL), BF16),
            pltpu.VMEM((2, FFN_HALO + tm, FC), F32),
            pltpu.VMEM((2, FFN_HALO + tm, FC), F32),
            pltpu.VMEM((tm, D_MODEL), F32),
        ],
        compiler_params=_params("parallel"),
        name="ffn",
    )(x, x, gpre, wup, cw, cb, wd, gpost)


def _rope_tables(seq):
    half = ROT_DIM // 2
    pos = jnp.arange(seq, dtype=F32)
    inv_freq = ROPE_THETA ** (-jnp.arange(0, ROT_DIM, 2, dtype=F32) / ROT_DIM)
    ang = pos[:, None] * inv_freq[None, :]
    cos, sin = jnp.cos(ang), jnp.sin(ang)
    ones = jnp.ones((seq, ATT_HD - ROT_DIM), F32)
    zeros_rest = jnp.zeros((seq, ATT_HD - ROT_DIM), F32)
    zeros_half = jnp.zeros((seq, half), F32)
    c = jnp.concatenate([cos, cos, ones], axis=1)
    a = jnp.concatenate([zeros_half, sin, zeros_rest], axis=1)
    b = jnp.concatenate([-sin, zeros_half, zeros_rest], axis=1)
    tile2 = lambda t: jnp.concatenate([t, t], axis=1)
    return tile2(c), tile2(a), tile2(b)


def kernel(x, pre_mix_norm, w_in, conv_w, conv_b, conv_ln_g, conv_ln_b, lambda_q1, lambda_k1,
           lambda_q2, lambda_k2, subln_g, w_out, post_mix_norm, pre_ffn_norm, w_up, ffn_conv_w,
           ffn_conv_b, w_down, post_ffn_norm):
    b, s, d = x.shape
    n = b * s
    rc, ra, rb = _rope_tables(s)
    xf = x.reshape(n, d)
    row = lambda t: t.reshape(1, -1)
    for l in range(DEPTH):
        lam_init = 0.8 - 0.6 * math.exp(-0.3 * l)
        lamc = jnp.zeros((1, LANES), F32).at[0, 0].set(lam_init).at[0, 1].set(1.0 - lam_init)
        u, q, k, vt = _in_proj(xf, row(pre_mix_norm[l]), w_in[l].astype(BF16), rc, ra, rb, s)
        u = _conv(u, conv_w[l], row(conv_b[l]), row(conv_ln_g[l]), row(conv_ln_b[l]), s)
        o = _attn(q.reshape(b, s, QK_COLS), k.reshape(b, s, QK_COLS), vt,
                  row(lambda_q1[l]), row(lambda_k1[l]), row(lambda_q2[l]), row(lambda_k2[l]),
                  lamc, row(subln_g[l]))
        xf = _out_proj(xf, u, o.reshape(n, ATT_OUT), w_out[l].astype(BF16),
                       row(post_mix_norm[l]))
        xf = _ffn(xf, row(pre_ffn_norm[l]), w_up[l].astype(BF16), ffn_conv_w[l],
                  row(ffn_conv_b[l]), w_down[l].astype(BF16), row(post_ffn_norm[l]), s)
    return xf.reshape(b, s, d)
```

```python
import functools
import math

import jax
import jax.numpy as jnp
from jax import lax
from jax.experimental import pallas as pl
from jax.experimental.pallas import tpu as pltpu

D_MODEL = 1024
DEPTH = 4
CONV_CH = 512
CONV_K = 31
ATT_HEADS = 4
ATT_HD = 64
ATT_VD = 2 * ATT_HD
ATT_OUT = ATT_HEADS * ATT_VD
QK_COLS = ATT_HEADS * 2 * ATT_HD
IN_COLS = 2 * CONV_CH + 2 * QK_COLS + ATT_OUT
ROT_DIM = ATT_HD // 4
ROPE_THETA = 500000.0
D_FF = 2816
FFN_K = 3
EPS = 1e-6

LANES = 128
SUBLANES = 8
VMEM_LIMIT = 56 * 1024 * 1024

TM_PROJ = 512
TS_CONV = 256
CONV_HALO = 32
CONV_ROWS = 32
TQ = 512
ATT_GROUP = 4
TM_FFN = 512
FFN_HALO = 16
FC = 256
NEG_BIG = -1e30

F32 = jnp.float32
BF16 = jnp.bfloat16


def _params(*sem):
    return pltpu.CompilerParams(dimension_semantics=sem, vmem_limit_bytes=VMEM_LIMIT)


def _rms(x, g):
    return x * lax.rsqrt(jnp.mean(x * x, axis=-1, keepdims=True) + EPS) * g


def _in_proj_kernel(x_ref, g_ref, w_ref, rc_ref, ra_ref, rb_ref,
                    u_ref, q_ref, k_ref, vt_ref):
    h = _rms(x_ref[...], g_ref[...]).astype(BF16)

    def proj(c0):
        return jnp.dot(h, w_ref[:, c0:c0 + CONV_CH], preferred_element_type=F32)

    za = proj(0)
    zg = proj(CONV_CH)
    u_ref[...] = za * jax.nn.sigmoid(zg)

    rc, ra, rb = rc_ref[...], ra_ref[...], rb_ref[...]

    def rope(z):
        return (z * rc + pltpu.roll(z, ROT_DIM // 2, 1) * ra
                + pltpu.roll(z, LANES - ROT_DIM // 2, 1) * rb)

    zq = proj(2 * CONV_CH)
    zk = proj(2 * CONV_CH + QK_COLS)
    scale = math.log2(math.e) / math.sqrt(ATT_HD)
    for hd in range(ATT_HEADS):
        sl = slice(hd * LANES, (hd + 1) * LANES)
        q_ref[:, sl] = (rope(zq[:, sl]) * scale).astype(BF16)
        k_ref[:, sl] = rope(zk[:, sl]).astype(BF16)
    vt_ref[0] = proj(2 * CONV_CH + 2 * QK_COLS).T.astype(BF16)


def _in_proj(x, g, w, rc, ra, rb, seq):
    n = x.shape[0]
    tm = TM_PROJ
    tiles_per_seq = seq // tm
    row = lambda i: (i, 0)
    fixed = lambda i: (0, 0)
    pos = lambda i: (i % tiles_per_seq, 0)
    return pl.pallas_call(
        _in_proj_kernel,
        grid=(n // tm,),
        in_specs=[
            pl.BlockSpec((tm, D_MODEL), row),
            pl.BlockSpec((1, D_MODEL), fixed),
            pl.BlockSpec((D_MODEL, IN_COLS), fixed),
            pl.BlockSpec((tm, LANES), pos),
            pl.BlockSpec((tm, LANES), pos),
            pl.BlockSpec((tm, LANES), pos),
        ],
        out_specs=[
            pl.BlockSpec((tm, CONV_CH), row),
            pl.BlockSpec((tm, QK_COLS), row),
            pl.BlockSpec((tm, QK_COLS), row),
            pl.BlockSpec((1, ATT_OUT, tm), lambda i: (i // tiles_per_seq, 0, i % tiles_per_seq)),
        ],
        out_shape=[
            jax.ShapeDtypeStruct((n, CONV_CH), F32),
            jax.ShapeDtypeStruct((n, QK_COLS), BF16),
            jax.ShapeDtypeStruct((n, QK_COLS), BF16),
            jax.ShapeDtypeStruct((n // seq, ATT_OUT, seq), BF16),
        ],
        compiler_params=_params("parallel"),
        name="in_proj",
    )(x, g, w, rc, ra, rb)


def _conv_kernel(tiles_per_seq, ucur_ref, uprev_ref, w_ref, b_ref, g_ref, beta_ref,
                 o_ref, sh_ref):
    first = (pl.program_id(0) % tiles_per_seq) == 0
    ts = ucur_ref.shape[0]
    sh_ref[0, 0:CONV_HALO, :] = jnp.where(first, 0.0, uprev_ref[...])
    sh_ref[0, CONV_HALO:, :] = ucur_ref[...]
    win = sh_ref[0]
    nwin = CONV_HALO + ts
    for r in range(1, SUBLANES):
        sh_ref[r] = pltpu.roll(win, nwin - r, 0)
    base = CONV_HALO - (CONV_K - 1)
    groups = CONV_ROWS // SUBLANES
    for r0 in range(0, ts, CONV_ROWS):
        acc = jnp.zeros((groups, SUBLANES, CONV_CH), F32)
        for k in range(CONV_K):
            phase = (base + k) % SUBLANES
            start = r0 + base + k - phase
            slab = sh_ref[phase, start:start + CONV_ROWS, :]
            acc = acc + w_ref[k] * slab.reshape(groups, SUBLANES, CONV_CH)
        acc = acc.reshape(CONV_ROWS, CONV_CH) + b_ref[...]
        mu = jnp.mean(acc, axis=-1, keepdims=True)
        d = acc - mu
        var = jnp.mean(d * d, axis=-1, keepdims=True)
        y = d * lax.rsqrt(var + EPS) * g_ref[...] + beta_ref[...]
        o_ref[r0:r0 + CONV_ROWS, :] = (y * jax.nn.sigmoid(y)).astype(BF16)


def _conv(u, w, b, g, beta, seq):
    n = u.shape[0]
    ts = TS_CONV
    tiles_per_seq = seq // ts
    halo_per_tile = ts // CONV_HALO
    row = lambda i: (i, 0)
    fixed = lambda i: (0, 0)
    prev = lambda i: (jnp.maximum(i * halo_per_tile - 1, 0), 0)
    w = jnp.broadcast_to(w[:, None, :], (CONV_K, SUBLANES, CONV_CH))
    return pl.pallas_call(
        functools.partial(_conv_kernel, tiles_per_seq),
        grid=(n // ts,),
        in_specs=[
            pl.BlockSpec((ts, CONV_CH), row),
            pl.BlockSpec((CONV_HALO, CONV_CH), prev),
            pl.BlockSpec((CONV_K, SUBLANES, CONV_CH), lambda i: (0, 0, 0)),
            pl.BlockSpec((1, CONV_CH), fixed),
            pl.BlockSpec((1, CONV_CH), fixed),
            pl.BlockSpec((1, CONV_CH), fixed),
        ],
        out_specs=pl.BlockSpec((ts, CONV_CH), row),
        out_shape=jax.ShapeDtypeStruct((n, CONV_CH), BF16),
        scratch_shapes=[pltpu.VMEM((SUBLANES, CONV_HALO + ts, CONV_CH), F32)],
        compiler_params=_params("parallel"),
        name="conv_ln_swish",
    )(u, u, w, b, g, beta)


def _attn_kernel(q_ref, k_ref, vt_ref, lq1_ref, lk1_ref, lq2_ref, lk2_ref, lamc_ref, sg_ref,
                 o_ref, sa_ref, sb_ref, xa_ref, xb_ref, m_ref, l_ref, acc_ref):
    qi = pl.program_id(2)
    tq = q_ref.shape[1]
    lane = lax.broadcasted_iota(jnp.int32, (tq, LANES), 1)
    qs = []
    for hh in range(ATT_GROUP):
        q = q_ref[0, :, hh * LANES:(hh + 1) * LANES]
        zero = jnp.zeros_like(q)
        qs += [jnp.where(lane < ATT_HD, q, zero), jnp.where(lane >= ATT_HD, q, zero)]

    m_ref[...] = jnp.full(m_ref.shape, NEG_BIG, F32)
    l_ref[...] = jnp.zeros(l_ref.shape, F32)
    acc_ref[...] = jnp.zeros(acc_ref.shape, F32)

    def scores(j, s_ref, x_ref):
        start = pl.multiple_of(j * tq, tq)
        for hh in range(ATT_GROUP):
            kb = k_ref[0, pl.ds(start, tq), hh * LANES:(hh + 1) * LANES]
            for n in (2 * hh, 2 * hh + 1):
                st = lax.dot_general(kb, qs[n], (((1,), (1,)), ((), ())),
                                     preferred_element_type=F32)
                s_ref[n] = st
                x_ref[n] = jnp.max(st, axis=0, keepdims=True)

    def softmax_pv(j, s_ref, x_ref, masked):
        start = pl.multiple_of(j * tq, tq)
        for hh in range(ATT_GROUP):
            vtb = vt_ref[0, hh * ATT_VD:(hh + 1) * ATT_VD, pl.ds(start, tq)]
            for n in (2 * hh, 2 * hh + 1):
                st = s_ref[n]
                if masked:
                    key = lax.broadcasted_iota(jnp.int32, st.shape, 0)
                    qry = lax.broadcasted_iota(jnp.int32, st.shape, 1)
                    st = jnp.where(key <= qry, st, NEG_BIG)
                    blk_max = jnp.max(st, axis=0, keepdims=True)
                else:
                    blk_max = x_ref[n]
                m_prev = m_ref[n]
                m_new = jnp.maximum(m_prev, blk_max)
                alpha = jnp.exp2(m_prev - m_new)
                pt = jnp.exp2(st - m_new)
                l_ref[n] = alpha * l_ref[n] + jnp.sum(pt, axis=0, keepdims=True)
                acc_ref[n] = alpha * acc_ref[n] + jnp.dot(vtb, pt.astype(BF16),
                                                          preferred_element_type=F32)
                m_ref[n] = m_new

    def by_parity(j, fn):
        lax.cond(j % 2 == 0, lambda: fn(sa_ref, xa_ref, sb_ref, xb_ref),
                 lambda: fn(sb_ref, xb_ref, sa_ref, xa_ref))

    scores(0, sa_ref, xa_ref)

    def body(j, carry):
        def run(s_cur, x_cur, s_nxt, x_nxt):
            scores(j + 1, s_nxt, x_nxt)
            softmax_pv(j, s_cur, x_cur, False)
        by_parity(j, run)
        return carry

    lax.fori_loop(0, qi, body, 0)
    by_parity(qi, lambda s_cur, x_cur, s_nxt, x_nxt: softmax_pv(qi, s_cur, x_cur, True))

    lam_init = lamc_ref[0:1, 0:1]
    out_scale = lamc_ref[0:1, 1:2]
    lam = (jnp.exp(jnp.sum(lq1_ref[...] * lk1_ref[...], axis=-1, keepdims=True))
           - jnp.exp(jnp.sum(lq2_ref[...] * lk2_ref[...], axis=-1, keepdims=True))
           + lam_init)
    for hh in range(ATT_GROUP):
        a, b = 2 * hh, 2 * hh + 1
        ot = acc_ref[a] / l_ref[a] - lam * (acc_ref[b] / l_ref[b])
        o_ref[0, :, hh * ATT_VD:(hh + 1) * ATT_VD] = (
            _rms(ot.T, sg_ref[...]) * out_scale).astype(BF16)


def _attn(q, k, vt, lq1, lk1, lq2, lk2, lamc, sg):
    b, s, _ = q.shape
    tq = TQ
    gw = ATT_GROUP * LANES
    streams = 2 * ATT_GROUP
    fixed = lambda bi, gi, qi: (0, 0)
    vec = pl.BlockSpec((1, ATT_HD), fixed)
    return pl.pallas_call(
        _attn_kernel,
        grid=(b, ATT_HEADS // ATT_GROUP, s // tq),
        in_specs=[
            pl.BlockSpec((1, tq, gw), lambda bi, gi, qi: (bi, qi, gi)),
            pl.BlockSpec((1, s, gw), lambda bi, gi, qi: (bi, 0, gi)),
            pl.BlockSpec((1, gw, s), lambda bi, gi, qi: (bi, gi, 0)),
            vec, vec, vec, vec,
            pl.BlockSpec((1, LANES), fixed),
            pl.BlockSpec((1, ATT_VD), fixed),
        ],
        out_specs=pl.BlockSpec((1, tq, gw), lambda bi, gi, qi: (bi, qi, gi)),
        out_shape=jax.ShapeDtypeStruct((b, s, ATT_OUT), BF16),
        scratch_shapes=[
            pltpu.VMEM((streams, tq, tq), F32),
            pltpu.VMEM((streams, tq, tq), F32),
            pltpu.VMEM((streams, 1, tq), F32),
            pltpu.VMEM((streams, 1, tq), F32),
            pltpu.VMEM((streams, 1, tq), F32),
            pltpu.VMEM((streams, 1, tq), F32),
            pltpu.VMEM((streams, ATT_VD, tq), F32),
        ],
        compiler_params=_params("parallel", "parallel", "parallel"),
        name="diff_attn",
    )(q, k, vt, lq1, lk1, lq2, lk2, lamc, sg)


def _out_proj_kernel(x_ref, u_ref, o_ref, w_ref, g_ref, y_ref):
    m = jnp.dot(u_ref[...], w_ref[0:CONV_CH, :], preferred_element_type=F32)
    m = m + jnp.dot(o_ref[...], w_ref[CONV_CH:, :], preferred_element_type=F32)
    y_ref[...] = x_ref[...] + _rms(m, g_ref[...])


def _out_proj(x, u, o, w, g):
    n = x.shape[0]
    tm = TM_PROJ
    row = lambda i: (i, 0)
    fixed = lambda i: (0, 0)
    return pl.pallas_call(
        _out_proj_kernel,
        grid=(n // tm,),
        in_specs=[
            pl.BlockSpec((tm, D_MODEL), row),
            pl.BlockSpec((tm, CONV_CH), row),
            pl.BlockSpec((tm, ATT_OUT), row),
            pl.BlockSpec((CONV_CH + ATT_OUT, D_MODEL), fixed),
            pl.BlockSpec((1, D_MODEL), fixed),
        ],
        out_specs=pl.BlockSpec((tm, D_MODEL), row),
        out_shape=jax.ShapeDtypeStruct((n, D_MODEL), F32),
        compiler_params=_params("parallel"),
        name="out_proj",
    )(x, u, o, w, g)


def _ffn_kernel(tiles_per_seq, x_ref, xprev_ref, gpre_ref, wup_ref, cw_ref, cb_ref, wd_ref,
                gpost_ref, y_ref, h_ref, ua_ref, ub_ref, acc_ref):
    tm = x_ref.shape[0]
    nf = wd_ref.shape[0]
    first = (pl.program_id(0) % tiles_per_seq) == 0
    hp = _rms(xprev_ref[...], gpre_ref[...])
    h_ref[0:FFN_HALO, :] = jnp.where(first, 0.0, hp).astype(BF16)
    h_ref[FFN_HALO:, :] = _rms(x_ref[...], gpre_ref[...]).astype(BF16)
    acc_ref[...] = jnp.zeros(acc_ref.shape, F32)

    def up_proj(j, u_ref):
        h = h_ref[...]
        u_ref[0] = jnp.dot(h, wup_ref[j], preferred_element_type=F32)
        u_ref[1] = jnp.dot(h, wup_ref[nf + j], preferred_element_type=F32)

    def process(j, u_ref):
        def conv(half, idx):
            out = cb_ref[idx]
            for t in range(FFN_K):
                off = FFN_HALO - (FFN_K - 1) + t
                out = out + cw_ref[idx, t:t + 1, :] * u_ref[half, off:off + tm, :]
            return out

        g = conv(0, j)
        val = conv(1, nf + j)
        act = (jax.nn.gelu(g, approximate=True) * val).astype(BF16)
        acc_ref[...] += jnp.dot(act, wd_ref[j], preferred_element_type=F32)

    up_proj(0, ua_ref)

    def body(p, carry):
        j = 2 * p
        up_proj(j + 1, ub_ref)
        process(j, ua_ref)
        up_proj(j + 2, ua_ref)
        process(j + 1, ub_ref)
        return carry

    lax.fori_loop(0, (nf - 1) // 2, body, 0)
    process(nf - 1, ua_ref)
    y_ref[...] = x_ref[...] + _rms(acc_ref[...], gpost_ref[...])


def _ffn(x, gpre, wup, cw, cb, wd, gpost, seq):
    n = x.shape[0]
    tm = TM_FFN
    nf = D_FF // FC
    assert nf % 2 == 1
    tiles_per_seq = seq // tm
    halo_per_tile = tm // FFN_HALO
    wup = wup.reshape(D_MODEL, 2 * nf, FC).transpose(1, 0, 2)
    cw = cw.reshape(FFN_K, 2 * nf, FC).transpose(1, 0, 2)
    cb = cb.reshape(2 * nf, 1, FC)
    wd = wd.reshape(nf, FC, D_MODEL)
    row = lambda i: (i, 0)
    prev = lambda i: (jnp.maximum(i * halo_per_tile - 1, 0), 0)

    def resident(shape):
        return pl.BlockSpec(shape, lambda i: (0,) * len(shape), pipeline_mode=pl.Buffered(1))

    return pl.pallas_call(
        functools.partial(_ffn_kernel, tiles_per_seq),
        grid=(n // tm,),
        in_specs=[
            pl.BlockSpec((tm, D_MODEL), row),
            pl.BlockSpec((FFN_HALO, D_MODEL), prev),
            resident((1, D_MODEL)),
            resident((2 * nf, D_MODEL, FC)),
            resident((2 * nf, FFN_K, FC)),
            resident((2 * nf, 1, FC)),
            resident((nf, FC, D_MODEL)),
            resident((1, D_MODEL)),
        ],
        out_specs=pl.BlockSpec((tm, D_MODEL), row),
        out_shape=jax.ShapeDtypeStruct((n, D_MODEL), F32),
        scratch_shapes=[
            pltpu.VMEM((FFN_HALO + tm, D_MODEL), BF16),
            pltpu.VMEM((2, FFN_HALO + tm, FC), F32),
            pltpu.VMEM((2, FFN_HALO + tm, FC), F32),
            pltpu.VMEM((tm, D_MODEL), F32),
        ],
        compiler_params=_params("parallel"),
        name="ffn",
    )(x, x, gpre, wup, cw, cb, wd, gpost)


def _rope_tables(seq):
    half = ROT_DIM // 2
    pos = jnp.arange(seq, dtype=F32)
    inv_freq = ROPE_THETA ** (-jnp.arange(0, ROT_DIM, 2, dtype=F32) / ROT_DIM)
    ang = pos[:, None] * inv_freq[None, :]
    cos, sin = jnp.cos(ang), jnp.sin(ang)
    ones = jnp.ones((seq, ATT_HD - ROT_DIM), F32)
    zeros_rest = jnp.zeros((seq, ATT_HD - ROT_DIM), F32)
    zeros_half = jnp.zeros((seq, half), F32)
    c = jnp.concatenate([cos, cos, ones], axis=1)
    a = jnp.concatenate([zeros_half, sin, zeros_rest], axis=1)
    b = jnp.concatenate([-sin, zeros_half, zeros_rest], axis=1)
    tile2 = lambda t: jnp.concatenate([t, t], axis=1)
    return tile2(c), tile2(a), tile2(b)


def kernel(x, pre_mix_norm, w_in, conv_w, conv_b, conv_ln_g, conv_ln_b, lambda_q1, lambda_k1,
           lambda_q2, lambda_k2, subln_g, w_out, post_mix_norm, pre_ffn_norm, w_up, ffn_conv_w,
           ffn_conv_b, w_down, post_ffn_norm):
    b, s, d = x.shape
    n = b * s
    rc, ra, rb = _rope_tables(s)
    xf = x.reshape(n, d)
    row = lambda t: t.reshape(1, -1)
    for l in range(DEPTH):
        lam_init = 0.8 - 0.6 * math.exp(-0.3 * l)
        lamc = jnp.zeros((1, LANES), F32).at[0, 0].set(lam_init).at[0, 1].set(1.0 - lam_init)
        u, q, k, vt = _in_proj(xf, row(pre_mix_norm[l]), w_in[l].astype(BF16), rc, ra, rb, s)
        u = _conv(u, conv_w[l], row(conv_b[l]), row(conv_ln_g[l]), row(conv_ln_b[l]), s)
        o = _attn(q.reshape(b, s, QK_COLS), k.reshape(b, s, QK_COLS), vt,
                  row(lambda_q1[l]), row(lambda_k1[l]), row(lambda_q2[l]), row(lambda_k2[l]),
                  lamc, row(subln_g[l]))
        xf = _out_proj(xf, u, o.reshape(n, ATT_OUT), w_out[l].astype(BF16),
                       row(post_mix_norm[l]))
        xf = _ffn(xf, row(pre_ffn_norm[l]), w_up[l].astype(BF16), ffn_conv_w[l],
                  row(ffn_conv_b[l]), w_down[l].astype(BF16), row(post_ffn_norm[l]), s)
    return xf.reshape(b, s, d)
```

```python
import functools
import math

import jax
import jax.numpy as jnp
from jax import lax
from jax.experimental import pallas as pl
from jax.experimental.pallas import tpu as pltpu

D_MODEL = 1024
DEPTH = 4
CONV_CH = 512
CONV_K = 31
ATT_HEADS = 4
ATT_HD = 64
ATT_VD = 2 * ATT_HD
ATT_OUT = ATT_HEADS * ATT_VD
QK_COLS = ATT_HEADS * 2 * ATT_HD
IN_COLS = 2 * CONV_CH + 2 * QK_COLS + ATT_OUT
ROT_DIM = ATT_HD // 4
ROPE_THETA = 500000.0
D_FF = 2816
FFN_K = 3
EPS = 1e-6

LANES = 128
SUBLANES = 8
VMEM_LIMIT = 56 * 1024 * 1024

TM_PROJ = 512
CONV_HALO = 32
CONV_ROWS = 32
TQ = 512
ATT_GROUP = 4
ATT_DEN_ROWS = 16
TM_FFN = 512
FFN_HALO = 16
FC = 256
NEG_BIG = -1e30
GELU_K1 = -2.0 * math.log2(math.e) * math.sqrt(2.0 / math.pi)
GELU_K3 = GELU_K1 * 0.044715

F32 = jnp.float32
BF16 = jnp.bfloat16


def _params(*sem):
    return pltpu.CompilerParams(dimension_semantics=sem, vmem_limit_bytes=VMEM_LIMIT)


def _rms(x, g):
    return x * lax.rsqrt(jnp.mean(x * x, axis=-1, keepdims=True) + EPS) * g


def _in_proj_kernel(tiles_per_seq, x_ref, xprev_ref, g_ref, w_ref, rc_ref, ra_ref, rb_ref,
                    cw_ref, cb_ref, lng_ref, lnb_ref, u_ref, q_ref, k_ref, vt_ref, sh_ref):
    first = (pl.program_id(0) % tiles_per_seq) == 0
    tm = x_ref.shape[0]
    h = _rms(x_ref[...], g_ref[...]).astype(BF16)
    hprev = _rms(xprev_ref[...], g_ref[...]).astype(BF16)
    hwin = jnp.concatenate([hprev, h], axis=0)

    def proj(lhs, c0):
        return jnp.dot(lhs, w_ref[:, c0:c0 + CONV_CH], preferred_element_type=F32)

    uwin = proj(hwin, 0) * jax.nn.sigmoid(proj(hwin, CONV_CH))
    sh_ref[0, 0:CONV_HALO, :] = jnp.where(first, 0.0, uwin[0:CONV_HALO])
    sh_ref[0, CONV_HALO:, :] = uwin[CONV_HALO:]
    win = sh_ref[0]
    nwin = CONV_HALO + tm
    for r in range(1, SUBLANES):
        sh_ref[r] = pltpu.roll(win, nwin - r, 0)
    base = CONV_HALO - (CONV_K - 1)
    groups = CONV_ROWS // SUBLANES

    def conv_rows(r0):
        acc = jnp.zeros((groups, SUBLANES, CONV_CH), F32)
        for k in range(CONV_K):
            phase = (base + k) % SUBLANES
            start = r0 + base + k - phase
            slab = sh_ref[phase, start:start + CONV_ROWS, :]
            acc = acc + cw_ref[k] * slab.reshape(groups, SUBLANES, CONV_CH)
        acc = acc.reshape(CONV_ROWS, CONV_CH) + cb_ref[...]
        mu = jnp.mean(acc, axis=-1, keepdims=True)
        d = acc - mu
        var = jnp.mean(d * d, axis=-1, keepdims=True)
        y = d * lax.rsqrt(var + EPS) * lng_ref[...] + lnb_ref[...]
        u_ref[r0:r0 + CONV_ROWS, :] = (y * jax.nn.sigmoid(y)).astype(BF16)

    rc, ra, rb = rc_ref[...], ra_ref[...], rb_ref[...]

    def rope(z):
        return (z * rc + pltpu.roll(z, ROT_DIM // 2, 1) * ra
                + pltpu.roll(z, LANES - ROT_DIM // 2, 1) * rb)

    for r0 in range(0, tm, CONV_ROWS):
        conv_rows(r0)

    zq = proj(h, 2 * CONV_CH)
    zk = proj(h, 2 * CONV_CH + QK_COLS)
    scale = math.log2(math.e) / math.sqrt(ATT_HD)
    for hd in range(ATT_HEADS):
        sl = slice(hd * LANES, (hd + 1) * LANES)
        q_ref[:, sl] = (rope(zq[:, sl]) * scale).astype(BF16)
        k_ref[:, sl] = rope(zk[:, sl]).astype(BF16)
    vt_ref[0] = proj(h, 2 * CONV_CH + 2 * QK_COLS).T.astype(BF16)


def _in_proj(x, g, w, rc, ra, rb, cw, cb, lng, lnb, seq):
    n = x.shape[0]
    tm = TM_PROJ
    tiles_per_seq = seq // tm
    halo_per_tile = tm // CONV_HALO
    row = lambda i: (i, 0)
    prev = lambda i: (jnp.maximum(i * halo_per_tile - 1, 0), 0)
    pos = lambda i: (i % tiles_per_seq, 0)
    cw = jnp.broadcast_to(cw[:, None, :], (CONV_K, SUBLANES, CONV_CH))

    def resident(shape):
        return pl.BlockSpec(shape, lambda i: (0,) * len(shape), pipeline_mode=pl.Buffered(1))

    return pl.pallas_call(
        functools.partial(_in_proj_kernel, tiles_per_seq),
        grid=(n // tm,),
        in_specs=[
            pl.BlockSpec((tm, D_MODEL), row),
            pl.BlockSpec((CONV_HALO, D_MODEL), prev),
            resident((1, D_MODEL)),
            resident((D_MODEL, IN_COLS)),
            pl.BlockSpec((tm, LANES), pos),
            pl.BlockSpec((tm, LANES), pos),
            pl.BlockSpec((tm, LANES), pos),
            resident((CONV_K, SUBLANES, CONV_CH)),
            resident((1, CONV_CH)),
            resident((1, CONV_CH)),
            resident((1, CONV_CH)),
        ],
        out_specs=[
            pl.BlockSpec((tm, CONV_CH), row),
            pl.BlockSpec((tm, QK_COLS), row),
            pl.BlockSpec((tm, QK_COLS), row),
            pl.BlockSpec((1, ATT_OUT, tm), lambda i: (i // tiles_per_seq, 0, i % tiles_per_seq)),
        ],
        out_shape=[
            jax.ShapeDtypeStruct((n, CONV_CH), BF16),
            jax.ShapeDtypeStruct((n, QK_COLS), BF16),
            jax.ShapeDtypeStruct((n, QK_COLS), BF16),
            jax.ShapeDtypeStruct((n // seq, ATT_OUT, seq), BF16),
        ],
        scratch_shapes=[pltpu.VMEM((SUBLANES, CONV_HALO + tm, CONV_CH), F32)],
        compiler_params=_params("parallel"),
        name="in_proj",
    )(x, x, g, w, rc, ra, rb, cw, cb, lng, lnb)


def _attn_kernel(q_ref, k_ref, vt_ref, lq1_ref, lk1_ref, lq2_ref, lk2_ref, lamc_ref, sg_ref,
                 o_ref, sa_ref, sb_ref, xa_ref, xb_ref, m_ref, acc_ref):
    qi = pl.program_id(2)
    tq = q_ref.shape[1]
    lane = lax.broadcasted_iota(jnp.int32, (tq, LANES), 1)
    qs = []
    for hh in range(ATT_GROUP):
        q = q_ref[0, :, hh * LANES:(hh + 1) * LANES]
        zero = jnp.zeros_like(q)
        qs += [jnp.where(lane < ATT_HD, q, zero), jnp.where(lane >= ATT_HD, q, zero)]

    m_ref[...] = jnp.full(m_ref.shape, NEG_BIG, F32)
    acc_ref[...] = jnp.zeros(acc_ref.shape, F32)
    ones_rows = jnp.ones((ATT_DEN_ROWS, tq), BF16)

    def scores(j, s_ref, x_ref):
        start = pl.multiple_of(j * tq, tq)
        for hh in range(ATT_GROUP):
            kb = k_ref[0, pl.ds(start, tq), hh * LANES:(hh + 1) * LANES]
            for n in (2 * hh, 2 * hh + 1):
                st = lax.dot_general(kb, qs[n], (((1,), (1,)), ((), ())),
                                     preferred_element_type=F32)
                s_ref[n] = st
                x_ref[n] = jnp.max(st, axis=0, keepdims=True)

    def softmax_pv(j, s_ref, x_ref, masked):
        start = pl.multiple_of(j * tq, tq)
        for hh in range(ATT_GROUP):
            vtb = vt_ref[0, hh * ATT_VD:(hh + 1) * ATT_VD, pl.ds(start, tq)]
            vtb = jnp.concatenate([vtb, ones_rows], axis=0)
            for n in (2 * hh, 2 * hh + 1):
                st = s_ref[n]
                if masked:
                    key = lax.broadcasted_iota(jnp.int32, st.shape, 0)
                    qry = lax.broadcasted_iota(jnp.int32, st.shape, 1)
                    st = jnp.where(key <= qry, st, NEG_BIG)
                    blk_max = jnp.max(st, axis=0, keepdims=True)
                else:
                    blk_max = x_ref[n]
                m_prev = m_ref[n]
                m_new = jnp.maximum(m_prev, blk_max)
                alpha = jnp.exp2(m_prev - m_new)
                pt = jnp.exp2(st - m_new)
                acc_ref[n] = alpha * acc_ref[n] + jnp.dot(vtb, pt.astype(BF16),
                                                          preferred_element_type=F32)
                m_ref[n] = m_new

    def by_parity(j, fn):
        lax.cond(j % 2 == 0, lambda: fn(sa_ref, xa_ref, sb_ref, xb_ref),
                 lambda: fn(sb_ref, xb_ref, sa_ref, xa_ref))

    scores(0, sa_ref, xa_ref)

    def body(j, carry):
        def run(s_cur, x_cur, s_nxt, x_nxt):
            scores(j + 1, s_nxt, x_nxt)
            softmax_pv(j, s_cur, x_cur, False)
        by_parity(j, run)
        return carry

    lax.fori_loop(0, qi, body, 0)
    by_parity(qi, lambda s_cur, x_cur, s_nxt, x_nxt: softmax_pv(qi, s_cur, x_cur, True))

    lam_init = lamc_ref[0:1, 0:1]
    out_scale = lamc_ref[0:1, 1:2]
    lam = (jnp.exp(jnp.sum(lq1_ref[...] * lk1_ref[...], axis=-1, keepdims=True))
           - jnp.exp(jnp.sum(lq2_ref[...] * lk2_ref[...], axis=-1, keepdims=True))
           + lam_init)
    def normalized(n):
        den = acc_ref[n, ATT_VD:ATT_VD + 1, :]
        return acc_ref[n, 0:ATT_VD, :] / den

    for hh in range(ATT_GROUP):
        ot = normalized(2 * hh) - lam * normalized(2 * hh + 1)
        o_ref[0, :, hh * ATT_VD:(hh + 1) * ATT_VD] = (
            _rms(ot.T, sg_ref[...]) * out_scale).astype(BF16)


def _attn(q, k, vt, lq1, lk1, lq2, lk2, lamc, sg):
    b, s, _ = q.shape
    tq = TQ
    gw = ATT_GROUP * LANES
    streams = 2 * ATT_GROUP
    fixed = lambda bi, gi, qi: (0, 0)
    vec = pl.BlockSpec((1, ATT_HD), fixed)
    return pl.pallas_call(
        _attn_kernel,
        grid=(b, ATT_HEADS // ATT_GROUP, s // tq),
        in_specs=[
            pl.BlockSpec((1, tq, gw), lambda bi, gi, qi: (bi, qi, gi)),
            pl.BlockSpec((1, s, gw), lambda bi, gi, qi: (bi, 0, gi)),
            pl.BlockSpec((1, gw, s), lambda bi, gi, qi: (bi, gi, 0)),
            vec, vec, vec, vec,
            pl.BlockSpec((1, LANES), fixed),
            pl.BlockSpec((1, ATT_VD), fixed),
        ],
        out_specs=pl.BlockSpec((1, tq, gw), lambda bi, gi, qi: (bi, qi, gi)),
        out_shape=jax.ShapeDtypeStruct((b, s, ATT_OUT), BF16),
        scratch_shapes=[
            pltpu.VMEM((streams, tq, tq), F32),
            pltpu.VMEM((streams, tq, tq), F32),
            pltpu.VMEM((streams, 1, tq), F32),
            pltpu.VMEM((streams, 1, tq), F32),
            pltpu.VMEM((streams, 1, tq), F32),
            pltpu.VMEM((streams, ATT_VD + ATT_DEN_ROWS, tq), F32),
        ],
        compiler_params=_params("parallel", "parallel", "parallel"),
        name="diff_attn",
    )(q, k, vt, lq1, lk1, lq2, lk2, lamc, sg)


def _out_proj_kernel(x_ref, u_ref, o_ref, w_ref, g_ref, y_ref):
    m = jnp.dot(u_ref[...], w_ref[0:CONV_CH, :], preferred_element_type=F32)
    m = m + jnp.dot(o_ref[...], w_ref[CONV_CH:, :], preferred_element_type=F32)
    y_ref[...] = x_ref[...] + _rms(m, g_ref[...])


def _out_proj(x, u, o, w, g):
    n = x.shape[0]
    tm = TM_PROJ
    row = lambda i: (i, 0)
    fixed = lambda i: (0, 0)
    return pl.pallas_call(
        _out_proj_kernel,
        grid=(n // tm,),
        in_specs=[
            pl.BlockSpec((tm, D_MODEL), row),
            pl.BlockSpec((tm, CONV_CH), row),
            pl.BlockSpec((tm, ATT_OUT), row),
            pl.BlockSpec((CONV_CH + ATT_OUT, D_MODEL), fixed),
            pl.BlockSpec((1, D_MODEL), fixed),
        ],
        out_specs=pl.BlockSpec((tm, D_MODEL), row),
        out_shape=jax.ShapeDtypeStruct((n, D_MODEL), F32),
        compiler_params=_params("parallel"),
        name="out_proj",
    )(x, u, o, w, g)


def _ffn_kernel(tiles_per_seq, x_ref, xprev_ref, gpre_ref, wup_ref, cw_ref, cb_ref, wd_ref,
                gpost_ref, y_ref, h_ref, ua_ref, ub_ref, acc_ref):
    tm = x_ref.shape[0]
    nf = wd_ref.shape[0]
    first = (pl.program_id(0) % tiles_per_seq) == 0
    hp = _rms(xprev_ref[...], gpre_ref[...])
    h_ref[0:FFN_HALO, :] = jnp.where(first, 0.0, hp).astype(BF16)
    h_ref[FFN_HALO:, :] = _rms(x_ref[...], gpre_ref[...]).astype(BF16)
    acc_ref[...] = jnp.zeros(acc_ref.shape, F32)

    def up_proj(j, u_ref):
        h = h_ref[...]
        u_ref[0] = jnp.dot(h, wup_ref[j], preferred_element_type=F32)
        u_ref[1] = jnp.dot(h, wup_ref[nf + j], preferred_element_type=F32)

    def process(j, u_ref):
        def conv(half, idx):
            out = cb_ref[idx]
            for t in range(FFN_K):
                off = FFN_HALO - (FFN_K - 1) + t
                out = out + cw_ref[idx, t:t + 1, :] * u_ref[half, off:off + tm, :]
            return out

        g = conv(0, j)
        val = conv(1, nf + j)
        e = jnp.exp2(g * (GELU_K1 + GELU_K3 * (g * g)))
        act = (g * val * (1.0 / (1.0 + e))).astype(BF16)
        acc_ref[...] += jnp.dot(act, wd_ref[j], preferred_element_type=F32)

    up_proj(0, ua_ref)

    def body(p, carry):
        j = 2 * p
        up_proj(j + 1, ub_ref)
        process(j, ua_ref)
        up_proj(j + 2, ua_ref)
        process(j + 1, ub_ref)
        return carry

    lax.fori_loop(0, (nf - 1) // 2, body, 0)
    process(nf - 1, ua_ref)
    y_ref[...] = x_ref[...] + _rms(acc_ref[...], gpost_ref[...])


def _ffn(x, gpre, wup, cw, cb, wd, gpost, seq):
    n = x.shape[0]
    tm = TM_FFN
    nf = D_FF // FC
    assert nf % 2 == 1
    tiles_per_seq = seq // tm
    halo_per_tile = tm // FFN_HALO
    wup = wup.reshape(D_MODEL, 2 * nf, FC).transpose(1, 0, 2)
    cw = cw.reshape(FFN_K, 2 * nf, FC).transpose(1, 0, 2)
    cb = cb.reshape(2 * nf, 1, FC)
    wd = wd.reshape(nf, FC, D_MODEL)
    row = lambda i: (i, 0)
    prev = lambda i: (jnp.maximum(i * halo_per_tile - 1, 0), 0)

    def resident(shape):
        return pl.BlockSpec(shape, lambda i: (0,) * len(shape), pipeline_mode=pl.Buffered(1))

    return pl.pallas_call(
        functools.partial(_ffn_kernel, tiles_per_seq),
        grid=(n // tm,),
        in_specs=[
            pl.BlockSpec((tm, D_MODEL), row),
            pl.BlockSpec((FFN_HALO, D_MODEL), prev),
            resident((1, D_MODEL)),
            resident((2 * nf, D_MODEL, FC)),
            resident((2 * nf, FFN_K, FC)),
            resident((2 * nf, 1, FC)),
            resident((nf, FC, D_MODEL)),
            resident((1, D_MODEL)),
        ],
        out_specs=pl.BlockSpec((tm, D_MODEL), row),
        out_shape=jax.ShapeDtypeStruct((n, D_MODEL), F32),
        scratch_shapes=[
            pltpu.VMEM((FFN_HALO + tm, D_MODEL), BF16),
            pltpu.VMEM((2, FFN_HALO + tm, FC), F32),
            pltpu.VMEM((2, FFN_HALO + tm, FC), F32),
            pltpu.VMEM((tm, D_MODEL), F32),
        ],
        compiler_params=_params("parallel"),
        name="ffn",
    )(x, x, gpre, wup, cw, cb, wd, gpost)


def _rope_tables(seq):
    half = ROT_DIM // 2
    pos = jnp.arange(seq, dtype=F32)
    inv_freq = ROPE_THETA ** (-jnp.arange(0, ROT_DIM, 2, dtype=F32) / ROT_DIM)
    ang = pos[:, None] * inv_freq[None, :]
    cos, sin = jnp.cos(ang), jnp.sin(ang)
    ones = jnp.ones((seq, ATT_HD - ROT_DIM), F32)
    zeros_rest = jnp.zeros((seq, ATT_HD - ROT_DIM), F32)
    zeros_half = jnp.zeros((seq, half), F32)
    c = jnp.concatenate([cos, cos, ones], axis=1)
    a = jnp.concatenate([zeros_half, sin, zeros_rest], axis=1)
    b = jnp.concatenate([-sin, zeros_half, zeros_rest], axis=1)
    tile2 = lambda t: jnp.concatenate([t, t], axis=1)
    return tile2(c), tile2(a), tile2(b)


def kernel(x, pre_mix_norm, w_in, conv_w, conv_b, conv_ln_g, conv_ln_b, lambda_q1, lambda_k1,
           lambda_q2, lambda_k2, subln_g, w_out, post_mix_norm, pre_ffn_norm, w_up, ffn_conv_w,
           ffn_conv_b, w_down, post_ffn_norm):
    b, s, d = x.shape
    n = b * s
    rc, ra, rb = _rope_tables(s)
    xf = x.reshape(n, d)
    row = lambda t: t.reshape(1, -1)
    for l in range(DEPTH):
        lam_init = 0.8 - 0.6 * math.exp(-0.3 * l)
        lamc = jnp.zeros((1, LANES), F32).at[0, 0].set(lam_init).at[0, 1].set(1.0 - lam_init)
        u, q, k, vt = _in_proj(xf, row(pre_mix_norm[l]), w_in[l].astype(BF16), rc, ra, rb,
                               conv_w[l], row(conv_b[l]), row(conv_ln_g[l]), row(conv_ln_b[l]), s)
        o = _attn(q.reshape(b, s, QK_COLS), k.reshape(b, s, QK_COLS), vt,
                  row(lambda_q1[l]), row(lambda_k1[l]), row(lambda_q2[l]), row(lambda_k2[l]),
                  lamc, row(subln_g[l]))
        xf = _out_proj(xf, u, o.reshape(n, ATT_OUT), w_out[l].astype(BF16),
                       row(post_mix_norm[l]))
        xf = _ffn(xf, row(pre_ffn_norm[l]), w_up[l].astype(BF16), ffn_conv_w[l],
                  row(ffn_conv_b[l]), w_down[l].astype(BF16), row(post_ffn_norm[l]), s)
    return xf.reshape(b, s, d)
```

```python
import functools
import math

import jax
import jax.numpy as jnp
from jax import lax
from jax.experimental import pallas as pl
from jax.experimental.pallas import tpu as pltpu

D_MODEL = 1024
DEPTH = 4
CONV_CH = 512
CONV_K = 31
ATT_HEADS = 4
ATT_HD = 64
ATT_VD = 2 * ATT_HD
ATT_OUT = ATT_HEADS * ATT_VD
QK_COLS = ATT_HEADS * 2 * ATT_HD
IN_COLS = 2 * CONV_CH + 2 * QK_COLS + ATT_OUT
ROT_DIM = ATT_HD // 4
ROPE_THETA = 500000.0
D_FF = 2816
FFN_K = 3
EPS = 1e-6

LANES = 128
SUBLANES = 8
VMEM_LIMIT = 56 * 1024 * 1024

TM_PROJ = 512
CONV_HALO = 32
CONV_ROWS = 32
TQ = 512
ATT_GROUP = 4
ATT_DEN_ROWS = 16
TM_FFN = 512
FFN_HALO = 16
FC = 256
NEG_BIG = -1e30
GELU_K1 = -2.0 * math.log2(math.e) * math.sqrt(2.0 / math.pi)
GELU_K3 = GELU_K1 * 0.044715

F32 = jnp.float32
BF16 = jnp.bfloat16


def _params(*sem):
    return pltpu.CompilerParams(dimension_semantics=sem, vmem_limit_bytes=VMEM_LIMIT)


def _rms(x, g):
    return x * lax.rsqrt(jnp.mean(x * x, axis=-1, keepdims=True) + EPS) * g


def _in_proj_kernel(tiles_per_seq, x_ref, xprev_ref, g_ref, w_ref, rc_ref, ra_ref, rb_ref,
                    cw_ref, cb_ref, lng_ref, lnb_ref, u_ref, q_ref, k_ref, vt_ref, sh_ref):
    first = (pl.program_id(0) % tiles_per_seq) == 0
    tm = x_ref.shape[0]
    h = _rms(x_ref[...], g_ref[...]).astype(BF16)
    hprev = _rms(xprev_ref[...], g_ref[...]).astype(BF16)
    hwin = jnp.concatenate([hprev, h], axis=0)

    def proj(lhs, c0):
        return jnp.dot(lhs, w_ref[:, c0:c0 + CONV_CH], preferred_element_type=F32)

    uwin = proj(hwin, 0) * jax.nn.sigmoid(proj(hwin, CONV_CH))
    sh_ref[0, 0:CONV_HALO, :] = jnp.where(first, 0.0, uwin[0:CONV_HALO])
    sh_ref[0, CONV_HALO:, :] = uwin[CONV_HALO:]
    win = sh_ref[0]
    nwin = CONV_HALO + tm
    for r in range(1, SUBLANES):
        sh_ref[r] = pltpu.roll(win, nwin - r, 0)
    base = CONV_HALO - (CONV_K - 1)
    groups = CONV_ROWS // SUBLANES

    def conv_rows(r0):
        acc = jnp.zeros((groups, SUBLANES, CONV_CH), F32)
        for k in range(CONV_K):
            phase = (base + k) % SUBLANES
            start = r0 + base + k - phase
            slab = sh_ref[phase, start:start + CONV_ROWS, :]
            acc = acc + cw_ref[k] * slab.reshape(groups, SUBLANES, CONV_CH)
        acc = acc.reshape(CONV_ROWS, CONV_CH) + cb_ref[...]
        mu = jnp.mean(acc, axis=-1, keepdims=True)
        d = acc - mu
        var = jnp.mean(d * d, axis=-1, keepdims=True)
        y = d * lax.rsqrt(var + EPS) * lng_ref[...] + lnb_ref[...]
        u_ref[r0:r0 + CONV_ROWS, :] = (y * jax.nn.sigmoid(y)).astype(BF16)

    rc, ra, rb = rc_ref[...], ra_ref[...], rb_ref[...]

    def rope(z):
        return (z * rc + pltpu.roll(z, ROT_DIM // 2, 1) * ra
                + pltpu.roll(z, LANES - ROT_DIM // 2, 1) * rb)

    for r0 in range(0, tm, CONV_ROWS):
        conv_rows(r0)

    zq = proj(h, 2 * CONV_CH)
    zk = proj(h, 2 * CONV_CH + QK_COLS)
    scale = math.log2(math.e) / math.sqrt(ATT_HD)
    for hd in range(ATT_HEADS):
        sl = slice(hd * LANES, (hd + 1) * LANES)
        q_ref[:, sl] = (rope(zq[:, sl]) * scale).astype(BF16)
        k_ref[:, sl] = rope(zk[:, sl]).astype(BF16)
    vt_ref[0] = proj(h, 2 * CONV_CH + 2 * QK_COLS).T.astype(BF16)


def _in_proj(x, g, w, rc, ra, rb, cw, cb, lng, lnb, seq):
    n = x.shape[0]
    tm = TM_PROJ
    tiles_per_seq = seq // tm
    halo_per_tile = tm // CONV_HALO
    row = lambda i: (i, 0)
    prev = lambda i: (jnp.maximum(i * halo_per_tile - 1, 0), 0)
    pos = lambda i: (i % tiles_per_seq, 0)
    cw = jnp.broadcast_to(cw[:, None, :], (CONV_K, SUBLANES, CONV_CH))

    def resident(shape):
        return pl.BlockSpec(shape, lambda i: (0,) * len(shape), pipeline_mode=pl.Buffered(1))

    return pl.pallas_call(
        functools.partial(_in_proj_kernel, tiles_per_seq),
        grid=(n // tm,),
        in_specs=[
            pl.BlockSpec((tm, D_MODEL), row),
            pl.BlockSpec((CONV_HALO, D_MODEL), prev),
            resident((1, D_MODEL)),
            resident((D_MODEL, IN_COLS)),
            pl.BlockSpec((tm, LANES), pos),
            pl.BlockSpec((tm, LANES), pos),
            pl.BlockSpec((tm, LANES), pos),
            resident((CONV_K, SUBLANES, CONV_CH)),
            resident((1, CONV_CH)),
            resident((1, CONV_CH)),
            resident((1, CONV_CH)),
        ],
        out_specs=[
            pl.BlockSpec((tm, CONV_CH), row),
            pl.BlockSpec((tm, QK_COLS), row),
            pl.BlockSpec((tm, QK_COLS), row),
            pl.BlockSpec((1, ATT_OUT, tm), lambda i: (i // tiles_per_seq, 0, i % tiles_per_seq)),
        ],
        out_shape=[
            jax.ShapeDtypeStruct((n, CONV_CH), BF16),
            jax.ShapeDtypeStruct((n, QK_COLS), BF16),
            jax.ShapeDtypeStruct((n, QK_COLS), BF16),
            jax.ShapeDtypeStruct((n // seq, ATT_OUT, seq), BF16),
        ],
        scratch_shapes=[pltpu.VMEM((SUBLANES, CONV_HALO + tm, CONV_CH), F32)],
        compiler_params=_params("parallel"),
        name="in_proj",
    )(x, x, g, w, rc, ra, rb, cw, cb, lng, lnb)


def _attn_kernel(q_ref, k_ref, vt_ref, lq1_ref, lk1_ref, lq2_ref, lk2_ref, lamc_ref, sg_ref,
                 o_ref, sa_ref, sb_ref, xa_ref, xb_ref, m_ref, acc_ref):
    qi = pl.program_id(2)
    tq = q_ref.shape[1]
    lane = lax.broadcasted_iota(jnp.int32, (tq, LANES), 1)
    qs = []
    for hh in range(ATT_GROUP):
        q = q_ref[0, :, hh * LANES:(hh + 1) * LANES]
        zero = jnp.zeros_like(q)
        qs += [jnp.where(lane < ATT_HD, q, zero), jnp.where(lane >= ATT_HD, q, zero)]

    m_ref[...] = jnp.full(m_ref.shape, NEG_BIG, F32)
    acc_ref[...] = jnp.zeros(acc_ref.shape, F32)
    ones_rows = jnp.ones((ATT_DEN_ROWS, tq), BF16)

    def scores(j, s_ref, x_ref):
        start = pl.multiple_of(j * tq, tq)
        for hh in range(ATT_GROUP):
            kb = k_ref[0, pl.ds(start, tq), hh * LANES:(hh + 1) * LANES]
            for n in (2 * hh, 2 * hh + 1):
                st = lax.dot_general(kb, qs[n], (((1,), (1,)), ((), ())),
                                     preferred_element_type=F32)
                s_ref[n] = st
                x_ref[n] = jnp.max(st, axis=0, keepdims=True)

    def softmax_pv(j, s_ref, x_ref, masked):
        start = pl.multiple_of(j * tq, tq)
        for hh in range(ATT_GROUP):
            vtb = vt_ref[0, hh * ATT_VD:(hh + 1) * ATT_VD, pl.ds(start, tq)]
            vtb = jnp.concatenate([vtb, ones_rows], axis=0)
            for n in (2 * hh, 2 * hh + 1):
                st = s_ref[n]
                if masked:
                    key = lax.broadcasted_iota(jnp.int32, st.shape, 0)
                    qry = lax.broadcasted_iota(jnp.int32, st.shape, 1)
                    st = jnp.where(key <= qry, st, NEG_BIG)
                    blk_max = jnp.max(st, axis=0, keepdims=True)
                else:
                    blk_max = x_ref[n]
                m_prev = m_ref[n]
                m_new = jnp.maximum(m_prev, blk_max)
                alpha = jnp.exp2(m_prev - m_new)
                pt = jnp.exp2(st - m_new)
                acc_ref[n] = alpha * acc_ref[n] + jnp.dot(vtb, pt.astype(BF16),
                                                          preferred_element_type=F32)
                m_ref[n] = m_new

    def by_parity(j, fn):
        lax.cond(j % 2 == 0, lambda: fn(sa_ref, xa_ref, sb_ref, xb_ref),
                 lambda: fn(sb_ref, xb_ref, sa_ref, xa_ref))

    scores(0, sa_ref, xa_ref)

    def body(j, carry):
        def run(s_cur, x_cur, s_nxt, x_nxt):
            scores(j + 1, s_nxt, x_nxt)
            softmax_pv(j, s_cur, x_cur, False)
        by_parity(j, run)
        return carry

    lax.fori_loop(0, qi, body, 0)
    by_parity(qi, lambda s_cur, x_cur, s_nxt, x_nxt: softmax_pv(qi, s_cur, x_cur, True))

    lam_init = lamc_ref[0:1, 0:1]
    out_scale = lamc_ref[0:1, 1:2]
    lam = (jnp.exp(jnp.sum(lq1_ref[...] * lk1_ref[...], axis=-1, keepdims=True))
           - jnp.exp(jnp.sum(lq2_ref[...] * lk2_ref[...], axis=-1, keepdims=True))
           + lam_init)
    def normalized(n):
        den = acc_ref[n, ATT_VD:ATT_VD + 1, :]
        return acc_ref[n, 0:ATT_VD, :] / den

    for hh in range(ATT_GROUP):
        ot = normalized(2 * hh) - lam * normalized(2 * hh + 1)
        o_ref[0, :, hh * ATT_VD:(hh + 1) * ATT_VD] = (
            _rms(ot.T, sg_ref[...]) * out_scale).astype(BF16)


def _attn(q, k, vt, lq1, lk1, lq2, lk2, lamc, sg):
    b, s, _ = q.shape
    tq = TQ
    gw = ATT_GROUP * LANES
    streams = 2 * ATT_GROUP
    fixed = lambda bi, gi, qi: (0, 0)
    vec = pl.BlockSpec((1, ATT_HD), fixed)
    return pl.pallas_call(
        _attn_kernel,
        grid=(b, ATT_HEADS // ATT_GROUP, s // tq),
        in_specs=[
            pl.BlockSpec((1, tq, gw), lambda bi, gi, qi: (bi, qi, gi)),
            pl.BlockSpec((1, s, gw), lambda bi, gi, qi: (bi, 0, gi)),
            pl.BlockSpec((1, gw, s), lambda bi, gi, qi: (bi, gi, 0)),
            vec, vec, vec, vec,
            pl.BlockSpec((1, LANES), fixed),
            pl.BlockSpec((1, ATT_VD), fixed),
        ],
        out_specs=pl.BlockSpec((1, tq, gw), lambda bi, gi, qi: (bi, qi, gi)),
        out_shape=jax.ShapeDtypeStruct((b, s, ATT_OUT), BF16),
        scratch_shapes=[
            pltpu.VMEM((streams, tq, tq), F32),
            pltpu.VMEM((streams, tq, tq), F32),
            pltpu.VMEM((streams, 1, tq), F32),
            pltpu.VMEM((streams, 1, tq), F32),
            pltpu.VMEM((streams, 1, tq), F32),
            pltpu.VMEM((streams, ATT_VD + ATT_DEN_ROWS, tq), F32),
        ],
        compiler_params=_params("parallel", "parallel", "parallel"),
        name="diff_attn",
    )(q, k, vt, lq1, lk1, lq2, lk2, lamc, sg)


def _out_proj_kernel(x_ref, u_ref, o_ref, w_ref, g_ref, y_ref):
    m = jnp.dot(u_ref[...], w_ref[0:CONV_CH, :], preferred_element_type=F32)
    m = m + jnp.dot(o_ref[...], w_ref[CONV_CH:, :], preferred_element_type=F32)
    y_ref[...] = x_ref[...] + _rms(m, g_ref[...])


def _out_proj(x, u, o, w, g):
    n = x.shape[0]
    tm = TM_PROJ
    row = lambda i: (i, 0)
    fixed = lambda i: (0, 0)
    return pl.pallas_call(
        _out_proj_kernel,
        grid=(n // tm,),
        in_specs=[
            pl.BlockSpec((tm, D_MODEL), row),
            pl.BlockSpec((tm, CONV_CH), row),
            pl.BlockSpec((tm, ATT_OUT), row),
            pl.BlockSpec((CONV_CH + ATT_OUT, D_MODEL), fixed),
            pl.BlockSpec((1, D_MODEL), fixed),
        ],
        out_specs=pl.BlockSpec((tm, D_MODEL), row),
        out_shape=jax.ShapeDtypeStruct((n, D_MODEL), F32),
        compiler_params=_params("parallel"),
        name="out_proj",
    )(x, u, o, w, g)


def _ffn_kernel(tiles_per_seq, x_ref, xprev_ref, gpre_ref, wup_ref, cw_ref, cb_ref, wd_ref,
                gpost_ref, y_ref, xs_ref, xp_ref, h_ref, ua_ref, ub_ref, act_ref, acc_ref):
    tm = x_ref.shape[0]
    ng = tm // SUBLANES
    nf = D_FF // FC
    first = (pl.program_id(0) % tiles_per_seq) == 0
    nlc = D_MODEL // LANES
    pitch = ng + SUBLANES
    for c in range(nlc):
        for s in range(SUBLANES):
            xs_ref[c, s * pitch:s * pitch + ng, :] = x_ref[s * ng:(s + 1) * ng,
                                                           c * LANES:(c + 1) * LANES]

    def strided_rows(g):
        return jnp.concatenate(
            [xs_ref[c, pl.ds(g, SUBLANES, stride=pitch), :] for c in range(nlc)], axis=1)

    for k in range(ng // 2):
        xp = jnp.concatenate([strided_rows(2 * k), strided_rows(2 * k + 1)], axis=0)
        xp_ref[2 * SUBLANES * k:2 * SUBLANES * (k + 1), :] = xp
        h_ref[2 * SUBLANES * k:2 * SUBLANES * (k + 1), :] = _rms(xp, gpre_ref[...]).astype(BF16)
    hp = _rms(xprev_ref[...], gpre_ref[...])
    h_ref[tm:, :] = jnp.where(first, 0.0, hp).astype(BF16)
    acc_ref[...] = jnp.zeros(acc_ref.shape, F32)
    sub0 = lax.broadcasted_iota(jnp.int32, (SUBLANES, FC), 0) == 0

    def cols(chunk):
        return pl.ds(pl.multiple_of(chunk * FC, FC), FC)

    def up_proj(j, u_ref):
        h = h_ref[...]
        u_ref[0] = jnp.dot(h, wup_ref[:, cols(j)], preferred_element_type=F32)
        u_ref[1] = jnp.dot(h, wup_ref[:, cols(nf + j)], preferred_element_type=F32)

    def process(j, u_ref, buf, slot):
        def conv(half, chunk):
            halo = u_ref[half, tm + FFN_HALO - SUBLANES:tm + FFN_HALO, :]
            edge = {}
            for m in range(1, FFN_K):
                tail = u_ref[half, tm - SUBLANES * m:tm - SUBLANES * (m - 1), :]
                edge[m] = jnp.where(sub0, pltpu.roll(halo, m, 0), pltpu.roll(tail, 1, 0))

            def lag(d):
                if d == 0:
                    return u_ref[half, 0:tm, :]
                groups = [edge[m] for m in range(d, 0, -1)]
                return jnp.concatenate(groups + [u_ref[half, 0:tm - SUBLANES * d, :]], axis=0)

            out = cb_ref[:, cols(chunk)]
            for t in range(FFN_K):
                out = out + cw_ref[t:t + 1, cols(chunk)] * lag(FFN_K - 1 - t)
            return out

        g = conv(0, j)
        val = conv(1, nf + j)
        e = jnp.exp2(g * (GELU_K1 + GELU_K3 * (g * g)))
        act_ref[buf, :, slot * FC:(slot + 1) * FC] = (
            g * val * (1.0 / (1.0 + e))).astype(BF16)

    def down_pair(p, buf):
        rows = pl.ds(pl.multiple_of(p * 2 * FC, 2 * FC), 2 * FC)
        acc_ref[...] += jnp.dot(act_ref[buf], wd_ref[rows, :], preferred_element_type=F32)

    def pair(p, buf, with_down):
        j = 2 * p
        up_proj(j + 1, ub_ref)
        if with_down:
            down_pair(p - 1, 1 - buf)
        process(j, ua_ref, buf, 0)
        up_proj(j + 2, ua_ref)
        process(j + 1, ub_ref, buf, 1)

    up_proj(0, ua_ref)
    pair(0, 0, False)

    def body(q, carry):
        pair(2 * q + 1, 1, True)
        pair(2 * q + 2, 0, True)
        return carry

    npairs = (nf - 1) // 2
    assert npairs % 2 == 1
    lax.fori_loop(0, (npairs - 1) // 2, body, 0)
    down_pair(npairs - 1, 0)
    process(nf - 1, ua_ref, 1, 0)
    acc = acc_ref[...] + jnp.dot(act_ref[1, :, 0:FC], wd_ref[(nf - 1) * FC:nf * FC, :],
                                 preferred_element_type=F32)
    yp = xp_ref[...] + _rms(acc, gpost_ref[...])
    for g in range(ng):
        for c in range(nlc):
            xs_ref[c, pl.ds(g, SUBLANES, stride=pitch), :] = (
                yp[SUBLANES * g:SUBLANES * (g + 1), c * LANES:(c + 1) * LANES])
    for c in range(nlc):
        for s in range(SUBLANES):
            y_ref[s * ng:(s + 1) * ng, c * LANES:(c + 1) * LANES] = (
                xs_ref[c, s * pitch:s * pitch + ng, :])


def _ffn(x, gpre, wup, cw, cb, wd, gpost, seq):
    n = x.shape[0]
    tm = TM_FFN
    nf = D_FF // FC
    assert nf % 2 == 1
    tiles_per_seq = seq // tm
    halo_per_tile = tm // FFN_HALO
    row = lambda i: (i, 0)
    prev = lambda i: (jnp.maximum(i * halo_per_tile - 1, 0), 0)

    def resident(shape):
        return pl.BlockSpec(shape, lambda i: (0,) * len(shape), pipeline_mode=pl.Buffered(1))

    return pl.pallas_call(
        functools.partial(_ffn_kernel, tiles_per_seq),
        grid=(n // tm,),
        in_specs=[
            pl.BlockSpec((tm, D_MODEL), row),
            pl.BlockSpec((FFN_HALO, D_MODEL), prev),
            resident((1, D_MODEL)),
            resident((D_MODEL, 2 * D_FF)),
            resident((FFN_K, 2 * D_FF)),
            resident((1, 2 * D_FF)),
            resident((D_FF, D_MODEL)),
            resident((1, D_MODEL)),
        ],
        out_specs=pl.BlockSpec((tm, D_MODEL), row),
        out_shape=jax.ShapeDtypeStruct((n, D_MODEL), F32),
        scratch_shapes=[
            pltpu.VMEM((D_MODEL // LANES, tm + SUBLANES * SUBLANES, LANES), F32),
            pltpu.VMEM((tm, D_MODEL), F32),
            pltpu.VMEM((FFN_HALO + tm, D_MODEL), BF16),
            pltpu.VMEM((2, FFN_HALO + tm, FC), F32),
            pltpu.VMEM((2, FFN_HALO + tm, FC), F32),
            pltpu.VMEM((2, tm, 2 * FC), BF16),
            pltpu.VMEM((tm, D_MODEL), F32),
        ],
        compiler_params=_params("parallel"),
        name="ffn",
    )(x, x, gpre, wup, cw, cb, wd, gpost)


def _rope_tables(seq):
    half = ROT_DIM // 2
    pos = jnp.arange(seq, dtype=F32)
    inv_freq = ROPE_THETA ** (-jnp.arange(0, ROT_DIM, 2, dtype=F32) / ROT_DIM)
    ang = pos[:, None] * inv_freq[None, :]
    cos, sin = jnp.cos(ang), jnp.sin(ang)
    ones = jnp.ones((seq, ATT_HD - ROT_DIM), F32)
    zeros_rest = jnp.zeros((seq, ATT_HD - ROT_DIM), F32)
    zeros_half = jnp.zeros((seq, half), F32)
    c = jnp.concatenate([cos, cos, ones], axis=1)
    a = jnp.concatenate([zeros_half, sin, zeros_rest], axis=1)
    b = jnp.concatenate([-sin, zeros_half, zeros_rest], axis=1)
    tile2 = lambda t: jnp.concatenate([t, t], axis=1)
    return tile2(c), tile2(a), tile2(b)


def kernel(x, pre_mix_norm, w_in, conv_w, conv_b, conv_ln_g, conv_ln_b, lambda_q1, lambda_k1,
           lambda_q2, lambda_k2, subln_g, w_out, post_mix_norm, pre_ffn_norm, w_up, ffn_conv_w,
           ffn_conv_b, w_down, post_ffn_norm):
    b, s, d = x.shape
    n = b * s
    rc, ra, rb = _rope_tables(s)
    xf = x.reshape(n, d)
    row = lambda t: t.reshape(1, -1)
    for l in range(DEPTH):
        lam_init = 0.8 - 0.6 * math.exp(-0.3 * l)
        lamc = jnp.zeros((1, LANES), F32).at[0, 0].set(lam_init).at[0, 1].set(1.0 - lam_init)
        u, q, k, vt = _in_proj(xf, row(pre_mix_norm[l]), w_in[l].astype(BF16), rc, ra, rb,
                               conv_w[l], row(conv_b[l]), row(conv_ln_g[l]), row(conv_ln_b[l]), s)
        o = _attn(q.reshape(b, s, QK_COLS), k.reshape(b, s, QK_COLS), vt,
                  row(lambda_q1[l]), row(lambda_k1[l]), row(lambda_q2[l]), row(lambda_k2[l]),
                  lamc, row(subln_g[l]))
        xf = _out_proj(xf, u, o.reshape(n, ATT_OUT), w_out[l].astype(BF16),
                       row(post_mix_norm[l]))
        xf = _ffn(xf, row(pre_ffn_norm[l]), w_up[l].astype(BF16), ffn_conv_w[l],
                  row(ffn_conv_b[l]), w_down[l].astype(BF16), row(post_ffn_norm[l]), s)
    return xf.reshape(b, s, d)
```

```python
import functools
import math

import jax
import jax.numpy as jnp
from jax import lax
from jax.experimental import pallas as pl
from jax.experimental.pallas import tpu as pltpu

D_MODEL = 1024
DEPTH = 4
CONV_CH = 512
CONV_K = 31
ATT_HEADS = 4
ATT_HD = 64
ATT_VD = 2 * ATT_HD
ATT_OUT = ATT_HEADS * ATT_VD
QK_COLS = ATT_HEADS * 2 * ATT_HD
IN_COLS = 2 * CONV_CH + 2 * QK_COLS + ATT_OUT
ROT_DIM = ATT_HD // 4
ROPE_THETA = 500000.0
D_FF = 2816
FFN_K = 3
EPS = 1e-6

LANES = 128
SUBLANES = 8
VMEM_LIMIT = 56 * 1024 * 1024

TM_PROJ = 512
CONV_HALO = 32
CONV_ROWS = 32
TQ = 512
ATT_GROUP = 4
ATT_DEN_ROWS = 16
TM_FFN = 512
FFN_HALO = 16
FC = 256
NEG_BIG = -1e30
GELU_K1 = -2.0 * math.log2(math.e) * math.sqrt(2.0 / math.pi)
GELU_K3 = GELU_K1 * 0.044715

F32 = jnp.float32
BF16 = jnp.bfloat16


def _params(*sem):
    return pltpu.CompilerParams(dimension_semantics=sem, vmem_limit_bytes=VMEM_LIMIT)


def _resident(shape):
    return pl.BlockSpec(shape, lambda *_: (0,) * len(shape), pipeline_mode=pl.Buffered(1))


def _layer_resident(layer, shape):
    return pl.BlockSpec((None,) + tuple(shape), lambda *_: (layer,) + (0,) * len(shape),
                        pipeline_mode=pl.Buffered(1))


def _rms(x, g):
    return x * lax.rsqrt(jnp.mean(x * x, axis=-1, keepdims=True) + EPS) * g


def _in_proj_kernel(tiles_per_seq, x_ref, xprev_ref, g_ref, w_ref, rc_ref, ra_ref, rb_ref,
                    cw_ref, cb_ref, lng_ref, lnb_ref, u_ref, q_ref, k_ref, vt_ref, sh_ref):
    first = (pl.program_id(0) % tiles_per_seq) == 0
    tm = x_ref.shape[0]
    h = _rms(x_ref[...], g_ref[...]).astype(BF16)
    hprev = _rms(xprev_ref[...], g_ref[...]).astype(BF16)
    hwin = jnp.concatenate([hprev, h], axis=0)

    def proj(lhs, c0):
        return jnp.dot(lhs, w_ref[:, c0:c0 + CONV_CH], preferred_element_type=F32)

    uwin = proj(hwin, 0) * jax.nn.sigmoid(proj(hwin, CONV_CH))
    sh_ref[0, 0:CONV_HALO, :] = jnp.where(first, 0.0, uwin[0:CONV_HALO])
    sh_ref[0, CONV_HALO:, :] = uwin[CONV_HALO:]
    win = sh_ref[0]
    nwin = CONV_HALO + tm
    for r in range(1, SUBLANES):
        sh_ref[r] = pltpu.roll(win, nwin - r, 0)
    base = CONV_HALO - (CONV_K - 1)
    groups = CONV_ROWS // SUBLANES

    def conv_rows(r0):
        acc = jnp.zeros((groups, SUBLANES, CONV_CH), F32)
        for k in range(CONV_K):
            phase = (base + k) % SUBLANES
            start = r0 + base + k - phase
            slab = sh_ref[phase, start:start + CONV_ROWS, :]
            acc = acc + cw_ref[k] * slab.reshape(groups, SUBLANES, CONV_CH)
        acc = acc.reshape(CONV_ROWS, CONV_CH) + cb_ref[...]
        mu = jnp.mean(acc, axis=-1, keepdims=True)
        d = acc - mu
        var = jnp.mean(d * d, axis=-1, keepdims=True)
        y = d * lax.rsqrt(var + EPS) * lng_ref[...] + lnb_ref[...]
        u_ref[r0:r0 + CONV_ROWS, :] = (y * jax.nn.sigmoid(y)).astype(BF16)

    rc, ra, rb = rc_ref[...], ra_ref[...], rb_ref[...]

    def rope(z):
        return (z * rc + pltpu.roll(z, ROT_DIM // 2, 1) * ra
                + pltpu.roll(z, LANES - ROT_DIM // 2, 1) * rb)

    for r0 in range(0, tm, CONV_ROWS):
        conv_rows(r0)

    zq = proj(h, 2 * CONV_CH)
    zk = proj(h, 2 * CONV_CH + QK_COLS)
    scale = math.log2(math.e) / math.sqrt(ATT_HD)
    for hd in range(ATT_HEADS):
        sl = slice(hd * LANES, (hd + 1) * LANES)
        q_ref[:, sl] = (rope(zq[:, sl]) * scale).astype(BF16)
        k_ref[:, sl] = rope(zk[:, sl]).astype(BF16)
    vt_ref[0] = proj(h, 2 * CONV_CH + 2 * QK_COLS).T.astype(BF16)


def _in_proj(x, g, w_all, layer, rc, ra, rb, cw, cb, lng, lnb, seq):
    n = x.shape[0]
    tm = TM_PROJ
    tiles_per_seq = seq // tm
    halo_per_tile = tm // CONV_HALO
    row = lambda i: (i, 0)
    prev = lambda i: (jnp.maximum(i * halo_per_tile - 1, 0), 0)
    pos = lambda i: (i % tiles_per_seq, 0)
    cw = jnp.broadcast_to(cw[:, None, :], (CONV_K, SUBLANES, CONV_CH))
    return pl.pallas_call(
        functools.partial(_in_proj_kernel, tiles_per_seq),
        grid=(n // tm,),
        in_specs=[
            pl.BlockSpec((tm, D_MODEL), row),
            pl.BlockSpec((CONV_HALO, D_MODEL), prev),
            _resident((1, D_MODEL)),
            _layer_resident(layer, (D_MODEL, IN_COLS)),
            pl.BlockSpec((tm, LANES), pos),
            pl.BlockSpec((tm, LANES), pos),
            pl.BlockSpec((tm, LANES), pos),
            _resident((CONV_K, SUBLANES, CONV_CH)),
            _resident((1, CONV_CH)),
            _resident((1, CONV_CH)),
            _resident((1, CONV_CH)),
        ],
        out_specs=[
            pl.BlockSpec((tm, CONV_CH), row),
            pl.BlockSpec((tm, QK_COLS), row),
            pl.BlockSpec((tm, QK_COLS), row),
            pl.BlockSpec((1, ATT_OUT, tm), lambda i: (i // tiles_per_seq, 0, i % tiles_per_seq)),
        ],
        out_shape=[
            jax.ShapeDtypeStruct((n, CONV_CH), BF16),
            jax.ShapeDtypeStruct((n, QK_COLS), BF16),
            jax.ShapeDtypeStruct((n, QK_COLS), BF16),
            jax.ShapeDtypeStruct((n // seq, ATT_OUT, seq), BF16),
        ],
        scratch_shapes=[pltpu.VMEM((SUBLANES, CONV_HALO + tm, CONV_CH), F32)],
        compiler_params=_params("parallel"),
        name="in_proj",
    )(x, x, g, w_all, rc, ra, rb, cw, cb, lng, lnb)


def _attn_kernel(q_ref, k_ref, vt_ref, lq1_ref, lk1_ref, lq2_ref, lk2_ref, lamc_ref, sg_ref,
                 o_ref, sa_ref, sb_ref, xa_ref, xb_ref, m_ref, acc_ref):
    qi = pl.program_id(2)
    tq = q_ref.shape[1]
    lane = lax.broadcasted_iota(jnp.int32, (tq, LANES), 1)
    qs = []
    for hh in range(ATT_GROUP):
        q = q_ref[0, :, hh * LANES:(hh + 1) * LANES]
        zero = jnp.zeros_like(q)
        qs += [jnp.where(lane < ATT_HD, q, zero), jnp.where(lane >= ATT_HD, q, zero)]

    m_ref[...] = jnp.full(m_ref.shape, NEG_BIG, F32)
    acc_ref[...] = jnp.zeros(acc_ref.shape, F32)
    ones_rows = jnp.ones((ATT_DEN_ROWS, tq), BF16)

    def scores(j, s_ref, x_ref):
        start = pl.multiple_of(j * tq, tq)
        for hh in range(ATT_GROUP):
            kb = k_ref[0, pl.ds(start, tq), hh * LANES:(hh + 1) * LANES]
            for n in (2 * hh, 2 * hh + 1):
                st = lax.dot_general(kb, qs[n], (((1,), (1,)), ((), ())),
                                     preferred_element_type=F32)
                s_ref[n] = st
                x_ref[n] = jnp.max(st, axis=0, keepdims=True)

    def softmax_pv(j, s_ref, x_ref, masked):
        start = pl.multiple_of(j * tq, tq)
        for hh in range(ATT_GROUP):
            vtb = vt_ref[0, hh * ATT_VD:(hh + 1) * ATT_VD, pl.ds(start, tq)]
            vtb = jnp.concatenate([vtb, ones_rows], axis=0)
            for n in (2 * hh, 2 * hh + 1):
                st = s_ref[n]
                if masked:
                    key = lax.broadcasted_iota(jnp.int32, st.shape, 0)
                    qry = lax.broadcasted_iota(jnp.int32, st.shape, 1)
                    st = jnp.where(key <= qry, st, NEG_BIG)
                    blk_max = jnp.max(st, axis=0, keepdims=True)
                else:
                    blk_max = x_ref[n]
                m_prev = m_ref[n]
                m_new = jnp.maximum(m_prev, blk_max)
                alpha = jnp.exp2(m_prev - m_new)
                pt = jnp.exp2(st - m_new)
                acc_ref[n] = alpha * acc_ref[n] + jnp.dot(vtb, pt.astype(BF16),
                                                          preferred_element_type=F32)
                m_ref[n] = m_new

    def by_parity(j, fn):
        lax.cond(j % 2 == 0, lambda: fn(sa_ref, xa_ref, sb_ref, xb_ref),
                 lambda: fn(sb_ref, xb_ref, sa_ref, xa_ref))

    scores(0, sa_ref, xa_ref)

    def body(j, carry):
        def run(s_cur, x_cur, s_nxt, x_nxt):
            scores(j + 1, s_nxt, x_nxt)
            softmax_pv(j, s_cur, x_cur, False)
        by_parity(j, run)
        return carry

    lax.fori_loop(0, qi, body, 0)
    by_parity(qi, lambda s_cur, x_cur, s_nxt, x_nxt: softmax_pv(qi, s_cur, x_cur, True))

    lam_init = lamc_ref[0:1, 0:1]
    out_scale = lamc_ref[0:1, 1:2]
    lam = (jnp.exp(jnp.sum(lq1_ref[...] * lk1_ref[...], axis=-1, keepdims=True))
           - jnp.exp(jnp.sum(lq2_ref[...] * lk2_ref[...], axis=-1, keepdims=True))
           + lam_init)

    def normalized(n):
        den = acc_ref[n, ATT_VD:ATT_VD + 1, :]
        return acc_ref[n, 0:ATT_VD, :] / den

    for hh in range(ATT_GROUP):
        ot = normalized(2 * hh) - lam * normalized(2 * hh + 1)
        o_ref[0, :, hh * ATT_VD:(hh + 1) * ATT_VD] = (
            _rms(ot.T, sg_ref[...]) * out_scale).astype(BF16)


def _attn(q, k, vt, lq1, lk1, lq2, lk2, lamc, sg):
    b, s, _ = q.shape
    tq = TQ
    gw = ATT_GROUP * LANES
    streams = 2 * ATT_GROUP
    fixed = lambda bi, gi, qi: (0, 0)
    vec = pl.BlockSpec((1, ATT_HD), fixed)
    return pl.pallas_call(
        _attn_kernel,
        grid=(b, ATT_HEADS // ATT_GROUP, s // tq),
        in_specs=[
            pl.BlockSpec((1, tq, gw), lambda bi, gi, qi: (bi, qi, gi)),
            pl.BlockSpec((1, s, gw), lambda bi, gi, qi: (bi, 0, gi)),
            pl.BlockSpec((1, gw, s), lambda bi, gi, qi: (bi, gi, 0)),
            vec, vec, vec, vec,
            pl.BlockSpec((1, LANES), fixed),
            pl.BlockSpec((1, ATT_VD), fixed),
        ],
        out_specs=pl.BlockSpec((1, tq, gw), lambda bi, gi, qi: (bi, qi, gi)),
        out_shape=jax.ShapeDtypeStruct((b, s, ATT_OUT), BF16),
        scratch_shapes=[
            pltpu.VMEM((streams, tq, tq), F32),
            pltpu.VMEM((streams, tq, tq), F32),
            pltpu.VMEM((streams, 1, tq), F32),
            pltpu.VMEM((streams, 1, tq), F32),
            pltpu.VMEM((streams, 1, tq), F32),
            pltpu.VMEM((streams, ATT_VD + ATT_DEN_ROWS, tq), F32),
        ],
        compiler_params=_params("parallel", "parallel", "parallel"),
        name="diff_attn",
    )(q, k, vt, lq1, lk1, lq2, lk2, lamc, sg)


def _out_proj_kernel(x_ref, u_ref, o_ref, w_ref, g_ref, y_ref):
    m = jnp.dot(u_ref[...], w_ref[0:CONV_CH, :], preferred_element_type=F32)
    m = m + jnp.dot(o_ref[...], w_ref[CONV_CH:, :], preferred_element_type=F32)
    y_ref[...] = x_ref[...] + _rms(m, g_ref[...])


def _out_proj(x, u, o, w_all, layer, g):
    n = x.shape[0]
    tm = TM_PROJ
    row = lambda i: (i, 0)
    return pl.pallas_call(
        _out_proj_kernel,
        grid=(n // tm,),
        in_specs=[
            pl.BlockSpec((tm, D_MODEL), row),
            pl.BlockSpec((tm, CONV_CH), row),
            pl.BlockSpec((tm, ATT_OUT), row),
            _layer_resident(layer, (CONV_CH + ATT_OUT, D_MODEL)),
            _resident((1, D_MODEL)),
        ],
        out_specs=pl.BlockSpec((tm, D_MODEL), row),
        out_shape=jax.ShapeDtypeStruct((n, D_MODEL), F32),
        compiler_params=_params("parallel"),
        name="out_proj",
    )(x, u, o, w_all, g)


def _ffn_kernel(tiles_per_seq, x_ref, xprev_ref, gpre_ref, wup_ref, cw_ref, cb_ref, wd_ref,
                gpost_ref, y_ref, xs_ref, xp_ref, h_ref, ua_ref, ub_ref, act_ref, acc_ref):
    tm = x_ref.shape[0]
    ng = tm // SUBLANES
    nf = D_FF // FC
    first = (pl.program_id(0) % tiles_per_seq) == 0
    nlc = D_MODEL // LANES
    pitch = ng + SUBLANES
    for c in range(nlc):
        for s in range(SUBLANES):
            xs_ref[c, s * pitch:s * pitch + ng, :] = x_ref[s * ng:(s + 1) * ng,
                                                           c * LANES:(c + 1) * LANES]

    def strided_rows(g):
        return jnp.concatenate(
            [xs_ref[c, pl.ds(g, SUBLANES, stride=pitch), :] for c in range(nlc)], axis=1)

    for k in range(ng // 2):
        xp = jnp.concatenate([strided_rows(2 * k), strided_rows(2 * k + 1)], axis=0)
        xp_ref[2 * SUBLANES * k:2 * SUBLANES * (k + 1), :] = xp
        h_ref[2 * SUBLANES * k:2 * SUBLANES * (k + 1), :] = _rms(xp, gpre_ref[...]).astype(BF16)
    hp = _rms(xprev_ref[...], gpre_ref[...])
    h_ref[tm:, :] = jnp.where(first, 0.0, hp).astype(BF16)
    acc_ref[...] = jnp.zeros(acc_ref.shape, F32)
    sub0 = lax.broadcasted_iota(jnp.int32, (SUBLANES, FC), 0) == 0

    def cols(chunk):
        return pl.ds(pl.multiple_of(chunk * FC, FC), FC)

    def up_proj(j, u_ref):
        h = h_ref[...]
        u_ref[0] = jnp.dot(h, wup_ref[:, cols(j)], preferred_element_type=F32)
        u_ref[1] = jnp.dot(h, wup_ref[:, cols(nf + j)], preferred_element_type=F32)

    def process(j, u_ref, buf, slot):
        def conv(half, chunk):
            halo = u_ref[half, tm + FFN_HALO - SUBLANES:tm + FFN_HALO, :]
            edge = {}
            for m in range(1, FFN_K):
                tail = u_ref[half, tm - SUBLANES * m:tm - SUBLANES * (m - 1), :]
                edge[m] = jnp.where(sub0, pltpu.roll(halo, m, 0), pltpu.roll(tail, 1, 0))

            def lag(d):
                if d == 0:
                    return u_ref[half, 0:tm, :]
                groups = [edge[m] for m in range(d, 0, -1)]
                return jnp.concatenate(groups + [u_ref[half, 0:tm - SUBLANES * d, :]], axis=0)

            out = cb_ref[:, cols(chunk)]
            for t in range(FFN_K):
                out = out + cw_ref[t:t + 1, cols(chunk)] * lag(FFN_K - 1 - t)
            return out

        g = conv(0, j)
        val = conv(1, nf + j)
        e = jnp.exp2(g * (GELU_K1 + GELU_K3 * (g * g)))
        act_ref[buf, :, slot * FC:(slot + 1) * FC] = (
            g * val * (1.0 / (1.0 + e))).astype(BF16)

    def down_pair(p, buf):
        rows = pl.ds(pl.multiple_of(p * 2 * FC, 2 * FC), 2 * FC)
        acc_ref[...] += jnp.dot(act_ref[buf], wd_ref[rows, :], preferred_element_type=F32)

    def pair(p, buf, with_down):
        j = 2 * p
        up_proj(j + 1, ub_ref)
        if with_down:
            down_pair(p - 1, 1 - buf)
        process(j, ua_ref, buf, 0)
        up_proj(j + 2, ua_ref)
        process(j + 1, ub_ref, buf, 1)

    up_proj(0, ua_ref)
    pair(0, 0, False)

    def body(q, carry):
        pair(2 * q + 1, 1, True)
        pair(2 * q + 2, 0, True)
        return carry

    npairs = (nf - 1) // 2
    assert npairs % 2 == 1
    lax.fori_loop(0, (npairs - 1) // 2, body, 0)
    down_pair(npairs - 1, 0)
    process(nf - 1, ua_ref, 1, 0)
    acc = acc_ref[...] + jnp.dot(act_ref[1, :, 0:FC], wd_ref[(nf - 1) * FC:nf * FC, :],
                                 preferred_element_type=F32)
    yp = xp_ref[...] + _rms(acc, gpost_ref[...])
    for g in range(ng):
        for c in range(nlc):
            xs_ref[c, pl.ds(g, SUBLANES, stride=pitch), :] = (
                yp[SUBLANES * g:SUBLANES * (g + 1), c * LANES:(c + 1) * LANES])
    for c in range(nlc):
        for s in range(SUBLANES):
            y_ref[s * ng:(s + 1) * ng, c * LANES:(c + 1) * LANES] = (
                xs_ref[c, s * pitch:s * pitch + ng, :])


def _ffn(x, gpre, wup_all, cw, cb, wd_all, layer, gpost, seq):
    n = x.shape[0]
    tm = TM_FFN
    nf = D_FF // FC
    assert nf % 2 == 1
    tiles_per_seq = seq // tm
    halo_per_tile = tm // FFN_HALO
    row = lambda i: (i, 0)
    prev = lambda i: (jnp.maximum(i * halo_per_tile - 1, 0), 0)
    return pl.pallas_call(
        functools.partial(_ffn_kernel, tiles_per_seq),
        grid=(n // tm,),
        in_specs=[
            pl.BlockSpec((tm, D_MODEL), row),
            pl.BlockSpec((FFN_HALO, D_MODEL), prev),
            _resident((1, D_MODEL)),
            _layer_resident(layer, (D_MODEL, 2 * D_FF)),
            _resident((FFN_K, 2 * D_FF)),
            _resident((1, 2 * D_FF)),
            _layer_resident(layer, (D_FF, D_MODEL)),
            _resident((1, D_MODEL)),
        ],
        out_specs=pl.BlockSpec((tm, D_MODEL), row),
        out_shape=jax.ShapeDtypeStruct((n, D_MODEL), F32),
        scratch_shapes=[
            pltpu.VMEM((D_MODEL // LANES, tm + SUBLANES * SUBLANES, LANES), F32),
            pltpu.VMEM((tm, D_MODEL), F32),
            pltpu.VMEM((FFN_HALO + tm, D_MODEL), BF16),
            pltpu.VMEM((2, FFN_HALO + tm, FC), F32),
            pltpu.VMEM((2, FFN_HALO + tm, FC), F32),
            pltpu.VMEM((2, tm, 2 * FC), BF16),
            pltpu.VMEM((tm, D_MODEL), F32),
        ],
        compiler_params=_params("parallel"),
        name="ffn",
    )(x, x, gpre, wup_all, cw, cb, wd_all, gpost)


def _rope_tables(seq):
    half = ROT_DIM // 2
    pos = jnp.arange(seq, dtype=F32)
    inv_freq = ROPE_THETA ** (-jnp.arange(0, ROT_DIM, 2, dtype=F32) / ROT_DIM)
    ang = pos[:, None] * inv_freq[None, :]
    cos, sin = jnp.cos(ang), jnp.sin(ang)
    ones = jnp.ones((seq, ATT_HD - ROT_DIM), F32)
    zeros_rest = jnp.zeros((seq, ATT_HD - ROT_DIM), F32)
    zeros_half = jnp.zeros((seq, half), F32)
    c = jnp.concatenate([cos, cos, ones], axis=1)
    a = jnp.concatenate([zeros_half, sin, zeros_rest], axis=1)
    b = jnp.concatenate([-sin, zeros_half, zeros_rest], axis=1)
    tile2 = lambda t: jnp.concatenate([t, t], axis=1)
    return tile2(c), tile2(a), tile2(b)


def kernel(x, pre_mix_norm, w_in, conv_w, conv_b, conv_ln_g, conv_ln_b, lambda_q1, lambda_k1,
           lambda_q2, lambda_k2, subln_g, w_out, post_mix_norm, pre_ffn_norm, w_up, ffn_conv_w,
           ffn_conv_b, w_down, post_ffn_norm):
    b, s, d = x.shape
    n = b * s
    rc, ra, rb = _rope_tables(s)
    xf = x.reshape(n, d)
    row = lambda t: t.reshape(1, -1)
    w_in, w_out, w_up, w_down = (w.astype(BF16) for w in (w_in, w_out, w_up, w_down))
    for l in range(DEPTH):
        lam_init = 0.8 - 0.6 * math.exp(-0.3 * l)
        lamc = jnp.zeros((1, LANES), F32).at[0, 0].set(lam_init).at[0, 1].set(1.0 - lam_init)
        u, q, k, vt = _in_proj(xf, row(pre_mix_norm[l]), w_in, l, rc, ra, rb,
                               conv_w[l], row(conv_b[l]), row(conv_ln_g[l]), row(conv_ln_b[l]), s)
        o = _attn(q.reshape(b, s, QK_COLS), k.reshape(b, s, QK_COLS), vt,
                  row(lambda_q1[l]), row(lambda_k1[l]), row(lambda_q2[l]), row(lambda_k2[l]),
                  lamc, row(subln_g[l]))
        xf = _out_proj(xf, u, o.reshape(n, ATT_OUT), w_out, l, row(post_mix_norm[l]))
        xf = _ffn(xf, row(pre_ffn_norm[l]), w_up, ffn_conv_w[l], row(ffn_conv_b[l]), w_down, l,
                  row(post_ffn_norm[l]), s)
    return xf.reshape(b, s, d)
```

```python
import functools
import math

import jax
import jax.numpy as jnp
from jax import lax
from jax.experimental import pallas as pl
from jax.experimental.pallas import tpu as pltpu

D_MODEL = 1024
DEPTH = 4
CONV_CH = 512
CONV_K = 31
ATT_HEADS = 4
ATT_HD = 64
ATT_VD = 2 * ATT_HD
ATT_OUT = ATT_HEADS * ATT_VD
QK_COLS = ATT_HEADS * 2 * ATT_HD
IN_COLS = 2 * CONV_CH + 2 * QK_COLS + ATT_OUT
ROT_DIM = ATT_HD // 4
ROPE_THETA = 500000.0
D_FF = 2816
FFN_K = 3
EPS = 1e-6

LANES = 128
SUBLANES = 8
VMEM_LIMIT = 56 * 1024 * 1024

TM_PROJ = 512
CONV_HALO = 32
CONV_ROWS = 32
TQ = 512
ATT_GROUP = 4
ATT_DEN_ROWS = 16
TM_FFN = 512
FFN_HALO = 16
FC = 256
NEG_BIG = -1e30
GELU_K1 = -2.0 * math.log2(math.e) * math.sqrt(2.0 / math.pi)
GELU_K3 = GELU_K1 * 0.044715

F32 = jnp.float32
BF16 = jnp.bfloat16


def _params(*sem):
    return pltpu.CompilerParams(dimension_semantics=sem, vmem_limit_bytes=VMEM_LIMIT)


def _resident(shape):
    return pl.BlockSpec(shape, lambda *_: (0,) * len(shape), pipeline_mode=pl.Buffered(1))


def _layer_resident(layer, shape):
    return pl.BlockSpec((None,) + tuple(shape), lambda *_: (layer,) + (0,) * len(shape),
                        pipeline_mode=pl.Buffered(1))


def _rms(x, g):
    return x * lax.rsqrt(jnp.mean(x * x, axis=-1, keepdims=True) + EPS) * g


def _in_proj_kernel(tiles_per_seq, x_ref, xprev_ref, g_ref, w_ref, rc_ref, ra_ref, rb_ref,
                    cw_ref, cb_ref, lng_ref, lnb_ref, u_ref, q_ref, k_ref, vt_ref, sh_ref):
    first = (pl.program_id(0) % tiles_per_seq) == 0
    tm = x_ref.shape[0]
    h = _rms(x_ref[...], g_ref[...]).astype(BF16)
    hprev = _rms(xprev_ref[...], g_ref[...]).astype(BF16)
    hwin = jnp.concatenate([hprev, h], axis=0)

    def proj(lhs, c0):
        return jnp.dot(lhs, w_ref[:, c0:c0 + CONV_CH], preferred_element_type=F32)

    uwin = proj(hwin, 0) * jax.nn.sigmoid(proj(hwin, CONV_CH))
    sh_ref[0, 0:CONV_HALO, :] = jnp.where(first, 0.0, uwin[0:CONV_HALO])
    sh_ref[0, CONV_HALO:, :] = uwin[CONV_HALO:]
    win = sh_ref[0]
    nwin = CONV_HALO + tm
    for r in range(1, SUBLANES):
        sh_ref[r] = pltpu.roll(win, nwin - r, 0)
    base = CONV_HALO - (CONV_K - 1)
    groups = CONV_ROWS // SUBLANES

    def conv_rows(r0):
        acc = jnp.zeros((groups, SUBLANES, CONV_CH), F32)
        for k in range(CONV_K):
            phase = (base + k) % SUBLANES
            start = r0 + base + k - phase
            slab = sh_ref[phase, start:start + CONV_ROWS, :]
            acc = acc + cw_ref[k] * slab.reshape(groups, SUBLANES, CONV_CH)
        acc = acc.reshape(CONV_ROWS, CONV_CH) + cb_ref[...]
        mu = jnp.mean(acc, axis=-1, keepdims=True)
        d = acc - mu
        var = jnp.mean(d * d, axis=-1, keepdims=True)
        y = d * lax.rsqrt(var + EPS) * lng_ref[...] + lnb_ref[...]
        u_ref[r0:r0 + CONV_ROWS, :] = (y * jax.nn.sigmoid(y)).astype(BF16)

    rc, ra, rb = rc_ref[...], ra_ref[...], rb_ref[...]

    def rope(z):
        return (z * rc + pltpu.roll(z, ROT_DIM // 2, 1) * ra
                + pltpu.roll(z, LANES - ROT_DIM // 2, 1) * rb)

    for r0 in range(0, tm, CONV_ROWS):
        conv_rows(r0)

    zq = proj(h, 2 * CONV_CH)
    zk = proj(h, 2 * CONV_CH + QK_COLS)
    scale = math.log2(math.e) / math.sqrt(ATT_HD)
    for hd in range(ATT_HEADS):
        sl = slice(hd * LANES, (hd + 1) * LANES)
        q_ref[:, sl] = (rope(zq[:, sl]) * scale).astype(BF16)
        k_ref[:, sl] = rope(zk[:, sl]).astype(BF16)
    vt_ref[0] = proj(h, 2 * CONV_CH + 2 * QK_COLS).T.astype(BF16)


def _in_proj(x, g, w_all, layer, rc, ra, rb, cw, cb, lng, lnb, seq):
    n = x.shape[0]
    tm = TM_PROJ
    tiles_per_seq = seq // tm
    halo_per_tile = tm // CONV_HALO
    row = lambda i: (i, 0)
    prev = lambda i: (jnp.maximum(i * halo_per_tile - 1, 0), 0)
    pos = lambda i: (i % tiles_per_seq, 0)
    cw = jnp.broadcast_to(cw[:, None, :], (CONV_K, SUBLANES, CONV_CH))
    return pl.pallas_call(
        functools.partial(_in_proj_kernel, tiles_per_seq),
        grid=(n // tm,),
        in_specs=[
            pl.BlockSpec((tm, D_MODEL), row),
            pl.BlockSpec((CONV_HALO, D_MODEL), prev),
            _resident((1, D_MODEL)),
            _layer_resident(layer, (D_MODEL, IN_COLS)),
            pl.BlockSpec((tm, LANES), pos),
            pl.BlockSpec((tm, LANES), pos),
            pl.BlockSpec((tm, LANES), pos),
            _resident((CONV_K, SUBLANES, CONV_CH)),
            _resident((1, CONV_CH)),
            _resident((1, CONV_CH)),
            _resident((1, CONV_CH)),
        ],
        out_specs=[
            pl.BlockSpec((tm, CONV_CH), row),
            pl.BlockSpec((tm, QK_COLS), row),
            pl.BlockSpec((tm, QK_COLS), row),
            pl.BlockSpec((1, ATT_OUT, tm), lambda i: (i // tiles_per_seq, 0, i % tiles_per_seq)),
        ],
        out_shape=[
            jax.ShapeDtypeStruct((n, CONV_CH), BF16),
            jax.ShapeDtypeStruct((n, QK_COLS), BF16),
            jax.ShapeDtypeStruct((n, QK_COLS), BF16),
            jax.ShapeDtypeStruct((n // seq, ATT_OUT, seq), BF16),
        ],
        scratch_shapes=[pltpu.VMEM((SUBLANES, CONV_HALO + tm, CONV_CH), F32)],
        compiler_params=_params("parallel"),
        name="in_proj",
    )(x, x, g, w_all, rc, ra, rb, cw, cb, lng, lnb)


def _attn_kernel(q_ref, k_ref, vt_ref, lq1_ref, lk1_ref, lq2_ref, lk2_ref, lamc_ref, sg_ref,
                 o_ref, sa_ref, sb_ref, xa_ref, xb_ref, m_ref, acc_ref):
    qi = pl.program_id(2)
    tq = q_ref.shape[1]
    lane = lax.broadcasted_iota(jnp.int32, (tq, LANES), 1)
    qs = []
    for hh in range(ATT_GROUP):
        q = q_ref[0, :, hh * LANES:(hh + 1) * LANES]
        zero = jnp.zeros_like(q)
        qs += [jnp.where(lane < ATT_HD, q, zero), jnp.where(lane >= ATT_HD, q, zero)]

    m_ref[...] = jnp.full(m_ref.shape, NEG_BIG, F32)
    acc_ref[...] = jnp.zeros(acc_ref.shape, F32)
    ones_rows = jnp.ones((ATT_DEN_ROWS, tq), BF16)

    def scores(j, s_ref, x_ref):
        start = pl.multiple_of(j * tq, tq)
        for hh in range(ATT_GROUP):
            kb = k_ref[0, pl.ds(start, tq), hh * LANES:(hh + 1) * LANES]
            for n in (2 * hh, 2 * hh + 1):
                st = lax.dot_general(kb, qs[n], (((1,), (1,)), ((), ())),
                                     preferred_element_type=F32)
                s_ref[n] = st
                x_ref[n] = jnp.max(st, axis=0, keepdims=True)

    def softmax_pv(j, s_ref, x_ref):
        start = pl.multiple_of(j * tq, tq)
        for hh in range(ATT_GROUP):
            vtb = vt_ref[0, hh * ATT_VD:(hh + 1) * ATT_VD, pl.ds(start, tq)]
            vtb = jnp.concatenate([vtb, ones_rows], axis=0)
            for n in (2 * hh, 2 * hh + 1):
                m_prev = m_ref[n]
                m_new = jnp.maximum(m_prev, x_ref[n])
                alpha = jnp.exp2(m_prev - m_new)
                pt = jnp.exp2(s_ref[n] - m_new)
                acc_ref[n] = alpha * acc_ref[n] + jnp.dot(vtb, pt.astype(BF16),
                                                          preferred_element_type=F32)
                m_ref[n] = m_new

    def softmax_pv_diagonal(j, s_ref):
        start = pl.multiple_of(j * tq, tq)
        hq = tq // 2
        key = lax.broadcasted_iota(jnp.int32, (hq, hq), 0)
        qry = lax.broadcasted_iota(jnp.int32, (hq, hq), 1)
        tri = key <= qry
        lo, hi = slice(0, hq), slice(hq, tq)
        for hh in range(ATT_GROUP):
            vtb = vt_ref[0, hh * ATT_VD:(hh + 1) * ATT_VD, pl.ds(start, tq)]
            vtb = jnp.concatenate([vtb, ones_rows], axis=0)
            for n in (2 * hh, 2 * hh + 1):
                s_a = jnp.where(tri, s_ref[n, lo, lo], NEG_BIG)
                s_b = s_ref[n, lo, hi]
                s_c = jnp.where(tri, s_ref[n, hi, hi], NEG_BIG)
                max_lo = jnp.max(s_a, axis=0, keepdims=True)
                max_hi = jnp.maximum(jnp.max(s_b, axis=0, keepdims=True),
                                     jnp.max(s_c, axis=0, keepdims=True))
                m_prev = m_ref[n]
                m_new = jnp.maximum(m_prev, jnp.concatenate([max_lo, max_hi], axis=1))
                alpha = jnp.exp2(m_prev - m_new)
                p_a = jnp.exp2(s_a - m_new[:, lo]).astype(BF16)
                p_bc = jnp.exp2(jnp.concatenate([s_b, s_c], axis=0) - m_new[:, hi]).astype(BF16)
                acc_ref[n, :, lo] = alpha[:, lo] * acc_ref[n, :, lo] + jnp.dot(
                    vtb[:, lo], p_a, preferred_element_type=F32)
                acc_ref[n, :, hi] = alpha[:, hi] * acc_ref[n, :, hi] + jnp.dot(
                    vtb, p_bc, preferred_element_type=F32)
                m_ref[n] = m_new

    def by_parity(j, fn):
        lax.cond(j % 2 == 0, lambda: fn(sa_ref, xa_ref, sb_ref, xb_ref),
                 lambda: fn(sb_ref, xb_ref, sa_ref, xa_ref))

    scores(0, sa_ref, xa_ref)

    def body(j, carry):
        def run(s_cur, x_cur, s_nxt, x_nxt):
            scores(j + 1, s_nxt, x_nxt)
            softmax_pv(j, s_cur, x_cur)
        by_parity(j, run)
        return carry

    lax.fori_loop(0, qi, body, 0)
    by_parity(qi, lambda s_cur, x_cur, s_nxt, x_nxt: softmax_pv_diagonal(qi, s_cur))

    lam_init = lamc_ref[0:1, 0:1]
    out_scale = lamc_ref[0:1, 1:2]
    lam = (jnp.exp(jnp.sum(lq1_ref[...] * lk1_ref[...], axis=-1, keepdims=True))
           - jnp.exp(jnp.sum(lq2_ref[...] * lk2_ref[...], axis=-1, keepdims=True))
           + lam_init)

    def normalized(n, weight):
        inv_den = weight / acc_ref[n, ATT_VD:ATT_VD + 1, :]
        return acc_ref[n, 0:ATT_VD, :] * inv_den

    for hh in range(ATT_GROUP):
        ot = normalized(2 * hh, 1.0) - normalized(2 * hh + 1, lam)
        inv = lax.rsqrt(jnp.mean(ot * ot, axis=0, keepdims=True) + EPS) * out_scale
        o_ref[0, :, hh * ATT_VD:(hh + 1) * ATT_VD] = (
            (ot * inv).T * sg_ref[...]).astype(BF16)


def _attn(q, k, vt, lq1, lk1, lq2, lk2, lamc, sg):
    b, s, _ = q.shape
    tq = TQ
    gw = ATT_GROUP * LANES
    streams = 2 * ATT_GROUP
    fixed = lambda bi, gi, qi: (0, 0)
    vec = pl.BlockSpec((1, ATT_HD), fixed)
    return pl.pallas_call(
        _attn_kernel,
        grid=(b, ATT_HEADS // ATT_GROUP, s // tq),
        in_specs=[
            pl.BlockSpec((1, tq, gw), lambda bi, gi, qi: (bi, qi, gi)),
            pl.BlockSpec((1, s, gw), lambda bi, gi, qi: (bi, 0, gi)),
            pl.BlockSpec((1, gw, s), lambda bi, gi, qi: (bi, gi, 0)),
            vec, vec, vec, vec,
            pl.BlockSpec((1, LANES), fixed),
            pl.BlockSpec((1, ATT_VD), fixed),
        ],
        out_specs=pl.BlockSpec((1, tq, gw), lambda bi, gi, qi: (bi, qi, gi)),
        out_shape=jax.ShapeDtypeStruct((b, s, ATT_OUT), BF16),
        scratch_shapes=[
            pltpu.VMEM((streams, tq, tq), F32),
            pltpu.VMEM((streams, tq, tq), F32),
            pltpu.VMEM((streams, 1, tq), F32),
            pltpu.VMEM((streams, 1, tq), F32),
            pltpu.VMEM((streams, 1, tq), F32),
            pltpu.VMEM((streams, ATT_VD + ATT_DEN_ROWS, tq), F32),
        ],
        compiler_params=_params("parallel", "parallel", "parallel"),
        name="diff_attn",
    )(q, k, vt, lq1, lk1, lq2, lk2, lamc, sg)


def _out_proj_kernel(x_ref, u_ref, o_ref, w_ref, g_ref, y_ref):
    m = jnp.dot(u_ref[...], w_ref[0:CONV_CH, :], preferred_element_type=F32)
    m = m + jnp.dot(o_ref[...], w_ref[CONV_CH:, :], preferred_element_type=F32)
    y_ref[...] = x_ref[...] + _rms(m, g_ref[...])


def _out_proj(x, u, o, w_all, layer, g):
    n = x.shape[0]
    tm = TM_PROJ
    row = lambda i: (i, 0)
    return pl.pallas_call(
        _out_proj_kernel,
        grid=(n // tm,),
        in_specs=[
            pl.BlockSpec((tm, D_MODEL), row),
            pl.BlockSpec((tm, CONV_CH), row),
            pl.BlockSpec((tm, ATT_OUT), row),
            _layer_resident(layer, (CONV_CH + ATT_OUT, D_MODEL)),
            _resident((1, D_MODEL)),
        ],
        out_specs=pl.BlockSpec((tm, D_MODEL), row),
        out_shape=jax.ShapeDtypeStruct((n, D_MODEL), F32),
        compiler_params=_params("parallel"),
        name="out_proj",
    )(x, u, o, w_all, g)


def _ffn_kernel(tiles_per_seq, x_ref, xprev_ref, gpre_ref, wup_ref, cw_ref, cb_ref, wd_ref,
                gpost_ref, y_ref, xs_ref, xp_ref, h_ref, ua_ref, ub_ref, act_ref, acc_ref):
    tm = x_ref.shape[0]
    ng = tm // SUBLANES
    nf = D_FF // FC
    first = (pl.program_id(0) % tiles_per_seq) == 0
    nlc = D_MODEL // LANES
    pitch = ng + SUBLANES
    for c in range(nlc):
        for s in range(SUBLANES):
            xs_ref[c, s * pitch:s * pitch + ng, :] = x_ref[s * ng:(s + 1) * ng,
                                                           c * LANES:(c + 1) * LANES]

    def strided_rows(g):
        return jnp.concatenate(
            [xs_ref[c, pl.ds(g, SUBLANES, stride=pitch), :] for c in range(nlc)], axis=1)

    for k in range(ng // 2):
        xp = jnp.concatenate([strided_rows(2 * k), strided_rows(2 * k + 1)], axis=0)
        xp_ref[2 * SUBLANES * k:2 * SUBLANES * (k + 1), :] = xp
        h_ref[2 * SUBLANES * k:2 * SUBLANES * (k + 1), :] = _rms(xp, gpre_ref[...]).astype(BF16)
    hp = _rms(xprev_ref[...], gpre_ref[...])
    h_ref[tm:, :] = jnp.where(first, 0.0, hp).astype(BF16)
    acc_ref[...] = jnp.zeros(acc_ref.shape, F32)
    sub0 = lax.broadcasted_iota(jnp.int32, (SUBLANES, FC), 0) == 0

    def cols(chunk):
        return pl.ds(pl.multiple_of(chunk * FC, FC), FC)

    def up_proj(j, u_ref):
        h = h_ref[...]
        u_ref[0] = jnp.dot(h, wup_ref[:, cols(j)], preferred_element_type=F32)
        u_ref[1] = jnp.dot(h, wup_ref[:, cols(nf + j)], preferred_element_type=F32)

    def process(j, u_ref, buf, slot):
        def conv(half, chunk):
            halo = u_ref[half, tm + FFN_HALO - SUBLANES:tm + FFN_HALO, :]
            edge = {}
            for m in range(1, FFN_K):
                tail = u_ref[half, tm - SUBLANES * m:tm - SUBLANES * (m - 1), :]
                edge[m] = jnp.where(sub0, pltpu.roll(halo, m, 0), pltpu.roll(tail, 1, 0))

            def lag(d):
                if d == 0:
                    return u_ref[half, 0:tm, :]
                groups = [edge[m] for m in range(d, 0, -1)]
                return jnp.concatenate(groups + [u_ref[half, 0:tm - SUBLANES * d, :]], axis=0)

            out = cb_ref[:, cols(chunk)]
            for t in range(FFN_K):
                out = out + cw_ref[t:t + 1, cols(chunk)] * lag(FFN_K - 1 - t)
            return out

        g = conv(0, j)
        val = conv(1, nf + j)
        e = jnp.exp2(g * (GELU_K1 + GELU_K3 * (g * g)))
        act_ref[buf, :, slot * FC:(slot + 1) * FC] = (
            g * val * (1.0 / (1.0 + e))).astype(BF16)

    def down_pair(p, buf):
        rows = pl.ds(pl.multiple_of(p * 2 * FC, 2 * FC), 2 * FC)
        acc_ref[...] += jnp.dot(act_ref[buf], wd_ref[rows, :], preferred_element_type=F32)

    def pair(p, buf, with_down):
        j = 2 * p
        up_proj(j + 1, ub_ref)
        if with_down:
            down_pair(p - 1, 1 - buf)
        process(j, ua_ref, buf, 0)
        up_proj(j + 2, ua_ref)
        process(j + 1, ub_ref, buf, 1)

    up_proj(0, ua_ref)
    pair(0, 0, False)

    def body(q, carry):
        pair(2 * q + 1, 1, True)
        pair(2 * q + 2, 0, True)
        return carry

    npairs = (nf - 1) // 2
    assert npairs % 2 == 1
    lax.fori_loop(0, (npairs - 1) // 2, body, 0)
    down_pair(npairs - 1, 0)
    process(nf - 1, ua_ref, 1, 0)
    acc = acc_ref[...] + jnp.dot(act_ref[1, :, 0:FC], wd_ref[(nf - 1) * FC:nf * FC, :],
                                 preferred_element_type=F32)
    yp = xp_ref[...] + _rms(acc, gpost_ref[...])
    for g in range(ng):
        for c in range(nlc):
            xs_ref[c, pl.ds(g, SUBLANES, stride=pitch), :] = (
                yp[SUBLANES * g:SUBLANES * (g + 1), c * LANES:(c + 1) * LANES])
    for c in range(nlc):
        for s in range(SUBLANES):
            y_ref[s * ng:(s + 1) * ng, c * LANES:(c + 1) * LANES] = (
                xs_ref[c, s * pitch:s * pitch + ng, :])


def _ffn(x, gpre, wup_all, cw, cb, wd_all, layer, gpost, seq):
    n = x.shape[0]
    tm = TM_FFN
    nf = D_FF // FC
    assert nf % 2 == 1
    tiles_per_seq = seq // tm
    halo_per_tile = tm // FFN_HALO
    row = lambda i: (i, 0)
    prev = lambda i: (jnp.maximum(i * halo_per_tile - 1, 0), 0)
    return pl.pallas_call(
        functools.partial(_ffn_kernel, tiles_per_seq),
        grid=(n // tm,),
        in_specs=[
            pl.BlockSpec((tm, D_MODEL), row),
            pl.BlockSpec((FFN_HALO, D_MODEL), prev),
            _resident((1, D_MODEL)),
            _layer_resident(layer, (D_MODEL, 2 * D_FF)),
            _resident((FFN_K, 2 * D_FF)),
            _resident((1, 2 * D_FF)),
            _layer_resident(layer, (D_FF, D_MODEL)),
            _resident((1, D_MODEL)),
        ],
        out_specs=pl.BlockSpec((tm, D_MODEL), row),
        out_shape=jax.ShapeDtypeStruct((n, D_MODEL), F32),
        scratch_shapes=[
            pltpu.VMEM((D_MODEL // LANES, tm + SUBLANES * SUBLANES, LANES), F32),
            pltpu.VMEM((tm, D_MODEL), F32),
            pltpu.VMEM((FFN_HALO + tm, D_MODEL), BF16),
            pltpu.VMEM((2, FFN_HALO + tm, FC), F32),
            pltpu.VMEM((2, FFN_HALO + tm, FC), F32),
            pltpu.VMEM((2, tm, 2 * FC), BF16),
            pltpu.VMEM((tm, D_MODEL), F32),
        ],
        compiler_params=_params("parallel"),
        name="ffn",
    )(x, x, gpre, wup_all, cw, cb, wd_all, gpost)


def _rope_tables(seq):
    half = ROT_DIM // 2
    pos = jnp.arange(seq, dtype=F32)
    inv_freq = ROPE_THETA ** (-jnp.arange(0, ROT_DIM, 2, dtype=F32) / ROT_DIM)
    ang = pos[:, None] * inv_freq[None, :]
    cos, sin = jnp.cos(ang), jnp.sin(ang)
    ones = jnp.ones((seq, ATT_HD - ROT_DIM), F32)
    zeros_rest = jnp.zeros((seq, ATT_HD - ROT_DIM), F32)
    zeros_half = jnp.zeros((seq, half), F32)
    c = jnp.concatenate([cos, cos, ones], axis=1)
    a = jnp.concatenate([zeros_half, sin, zeros_rest], axis=1)
    b = jnp.concatenate([-sin, zeros_half, zeros_rest], axis=1)
    tile2 = lambda t: jnp.concatenate([t, t], axis=1)
    return tile2(c), tile2(a), tile2(b)


def kernel(x, pre_mix_norm, w_in, conv_w, conv_b, conv_ln_g, conv_ln_b, lambda_q1, lambda_k1,
           lambda_q2, lambda_k2, subln_g, w_out, post_mix_norm, pre_ffn_norm, w_up, ffn_conv_w,
           ffn_conv_b, w_down, post_ffn_norm):
    b, s, d = x.shape
    n = b * s
    rc, ra, rb = _rope_tables(s)
    xf = x.reshape(n, d)
    row = lambda t: t.reshape(1, -1)
    w_in, w_out, w_up, w_down = (w.astype(BF16) for w in (w_in, w_out, w_up, w_down))
    for l in range(DEPTH):
        lam_init = 0.8 - 0.6 * math.exp(-0.3 * l)
        lamc = jnp.zeros((1, LANES), F32).at[0, 0].set(lam_init).at[0, 1].set(1.0 - lam_init)
        u, q, k, vt = _in_proj(xf, row(pre_mix_norm[l]), w_in, l, rc, ra, rb,
                               conv_w[l], row(conv_b[l]), row(conv_ln_g[l]), row(conv_ln_b[l]), s)
        o = _attn(q.reshape(b, s, QK_COLS), k.reshape(b, s, QK_COLS), vt,
                  row(lambda_q1[l]), row(lambda_k1[l]), row(lambda_q2[l]), row(lambda_k2[l]),
                  lamc, row(subln_g[l]))
        xf = _out_proj(xf, u, o.reshape(n, ATT_OUT), w_out, l, row(post_mix_norm[l]))
        xf = _ffn(xf, row(pre_ffn_norm[l]), w_up, ffn_conv_w[l], row(ffn_conv_b[l]), w_down, l,
                  row(post_ffn_norm[l]), s)
    return xf.reshape(b, s, d)
```

```python
import functools
import math

import jax
import jax.numpy as jnp
from jax import lax
from jax.experimental import pallas as pl
from jax.experimental.pallas import tpu as pltpu

D_MODEL = 1024
DEPTH = 4
CONV_CH = 512
CONV_K = 31
ATT_HEADS = 4
ATT_HD = 64
ATT_VD = 2 * ATT_HD
ATT_OUT = ATT_HEADS * ATT_VD
QK_COLS = ATT_HEADS * 2 * ATT_HD
IN_COLS = 2 * CONV_CH + 2 * QK_COLS + ATT_OUT
ROT_DIM = ATT_HD // 4
ROPE_THETA = 500000.0
D_FF = 2816
FFN_K = 3
EPS = 1e-6

LANES = 128
SUBLANES = 8
VMEM_LIMIT = 56 * 1024 * 1024

TM_PROJ = 1024
CONV_HALO = 32
CONV_ROWS = 32
TQ = 512
ATT_GROUP = 4
ATT_DEN_ROWS = 16
TM_FFN = 512
FFN_HALO = 16
FC = 256
NEG_BIG = -1e30
GELU_K1 = -2.0 * math.log2(math.e) * math.sqrt(2.0 / math.pi)
GELU_K3 = GELU_K1 * 0.044715

F32 = jnp.float32
BF16 = jnp.bfloat16


def _params(*sem):
    return pltpu.CompilerParams(dimension_semantics=sem, vmem_limit_bytes=VMEM_LIMIT)


def _resident(shape):
    return pl.BlockSpec(shape, lambda *_: (0,) * len(shape), pipeline_mode=pl.Buffered(1))


def _layer_resident(layer, shape):
    return pl.BlockSpec((None,) + tuple(shape), lambda *_: (layer,) + (0,) * len(shape),
                        pipeline_mode=pl.Buffered(1))


def _rms(x, g):
    return x * lax.rsqrt(jnp.mean(x * x, axis=-1, keepdims=True) + EPS) * g


def _in_proj_kernel(tiles_per_seq, x_ref, xprev_ref, g_ref, w_ref, rc_ref, ra_ref, rb_ref,
                    cw_ref, cb_ref, lng_ref, lnb_ref, u_ref, q_ref, k_ref, vt_ref, sh_ref):
    first = (pl.program_id(0) % tiles_per_seq) == 0
    tm = x_ref.shape[0]
    h = _rms(x_ref[...], g_ref[...]).astype(BF16)
    hprev = _rms(xprev_ref[...], g_ref[...]).astype(BF16)
    hwin = jnp.concatenate([hprev, h], axis=0)

    def proj(lhs, c0):
        return jnp.dot(lhs, w_ref[:, c0:c0 + CONV_CH], preferred_element_type=F32)

    uwin = proj(hwin, 0) * jax.nn.sigmoid(proj(hwin, CONV_CH))
    sh_ref[0, 0:CONV_HALO, :] = jnp.where(first, 0.0, uwin[0:CONV_HALO])
    sh_ref[0, CONV_HALO:, :] = uwin[CONV_HALO:]
    win = sh_ref[0]
    nwin = CONV_HALO + tm
    for r in range(1, SUBLANES):
        sh_ref[r] = pltpu.roll(win, nwin - r, 0)
    base = CONV_HALO - (CONV_K - 1)
    groups = CONV_ROWS // SUBLANES

    def conv_rows(r0):
        acc = jnp.zeros((groups, SUBLANES, CONV_CH), F32)
        for k in range(CONV_K):
            phase = (base + k) % SUBLANES
            start = r0 + base + k - phase
            slab = sh_ref[phase, start:start + CONV_ROWS, :]
            acc = acc + cw_ref[k] * slab.reshape(groups, SUBLANES, CONV_CH)
        acc = acc.reshape(CONV_ROWS, CONV_CH) + cb_ref[...]
        mu = jnp.mean(acc, axis=-1, keepdims=True)
        d = acc - mu
        var = jnp.mean(d * d, axis=-1, keepdims=True)
        y = d * lax.rsqrt(var + EPS) * lng_ref[...] + lnb_ref[...]
        u_ref[r0:r0 + CONV_ROWS, :] = (y * jax.nn.sigmoid(y)).astype(BF16)

    rc, ra, rb = rc_ref[...], ra_ref[...], rb_ref[...]

    def rope(z):
        return (z * rc + pltpu.roll(z, ROT_DIM // 2, 1) * ra
                + pltpu.roll(z, LANES - ROT_DIM // 2, 1) * rb)

    for r0 in range(0, tm, CONV_ROWS):
        conv_rows(r0)

    zq = proj(h, 2 * CONV_CH)
    zk = proj(h, 2 * CONV_CH + QK_COLS)
    scale = math.log2(math.e) / math.sqrt(ATT_HD)
    for hd in range(ATT_HEADS):
        sl = slice(hd * LANES, (hd + 1) * LANES)
        q_ref[:, sl] = (rope(zq[:, sl]) * scale).astype(BF16)
        k_ref[:, sl] = rope(zk[:, sl]).astype(BF16)
    vt_ref[0] = proj(h, 2 * CONV_CH + 2 * QK_COLS).T.astype(BF16)


def _in_proj(x, g, w_all, layer, rc, ra, rb, cw, cb, lng, lnb, seq):
    n = x.shape[0]
    tm = TM_PROJ
    tiles_per_seq = seq // tm
    halo_per_tile = tm // CONV_HALO
    row = lambda i: (i, 0)
    prev = lambda i: (jnp.maximum(i * halo_per_tile - 1, 0), 0)
    pos = lambda i: (i % tiles_per_seq, 0)
    cw = jnp.broadcast_to(cw[:, None, :], (CONV_K, SUBLANES, CONV_CH))
    return pl.pallas_call(
        functools.partial(_in_proj_kernel, tiles_per_seq),
        grid=(n // tm,),
        in_specs=[
            pl.BlockSpec((tm, D_MODEL), row),
            pl.BlockSpec((CONV_HALO, D_MODEL), prev),
            _resident((1, D_MODEL)),
            _layer_resident(layer, (D_MODEL, IN_COLS)),
            pl.BlockSpec((tm, LANES), pos),
            pl.BlockSpec((tm, LANES), pos),
            pl.BlockSpec((tm, LANES), pos),
            _resident((CONV_K, SUBLANES, CONV_CH)),
            _resident((1, CONV_CH)),
            _resident((1, CONV_CH)),
            _resident((1, CONV_CH)),
        ],
        out_specs=[
            pl.BlockSpec((tm, CONV_CH), row),
            pl.BlockSpec((tm, QK_COLS), row),
            pl.BlockSpec((tm, QK_COLS), row),
            pl.BlockSpec((1, ATT_OUT, tm), lambda i: (i // tiles_per_seq, 0, i % tiles_per_seq)),
        ],
        out_shape=[
            jax.ShapeDtypeStruct((n, CONV_CH), BF16),
            jax.ShapeDtypeStruct((n, QK_COLS), BF16),
            jax.ShapeDtypeStruct((n, QK_COLS), BF16),
            jax.ShapeDtypeStruct((n // seq, ATT_OUT, seq), BF16),
        ],
        scratch_shapes=[pltpu.VMEM((SUBLANES, CONV_HALO + tm, CONV_CH), F32)],
        compiler_params=_params("parallel"),
        name="in_proj",
    )(x, x, g, w_all, rc, ra, rb, cw, cb, lng, lnb)


def _attn_kernel(q_ref, k_ref, vt_ref, lq1_ref, lk1_ref, lq2_ref, lk2_ref, lamc_ref, sg_ref,
                 o_ref, sa_ref, sb_ref, xa_ref, xb_ref, m_ref, acc_ref):
    qi = pl.program_id(2)
    tq = q_ref.shape[1]
    lane = lax.broadcasted_iota(jnp.int32, (tq, LANES), 1)
    qs = []
    for hh in range(ATT_GROUP):
        q = q_ref[0, :, hh * LANES:(hh + 1) * LANES]
        zero = jnp.zeros_like(q)
        qs += [jnp.where(lane < ATT_HD, q, zero), jnp.where(lane >= ATT_HD, q, zero)]

    m_ref[...] = jnp.full(m_ref.shape, NEG_BIG, F32)
    acc_ref[...] = jnp.zeros(acc_ref.shape, F32)
    ones_rows = jnp.ones((ATT_DEN_ROWS, tq), BF16)

    def scores(j, s_ref, x_ref):
        start = pl.multiple_of(j * tq, tq)
        for hh in range(ATT_GROUP):
            kb = k_ref[0, pl.ds(start, tq), hh * LANES:(hh + 1) * LANES]
            for n in (2 * hh, 2 * hh + 1):
                st = lax.dot_general(kb, qs[n], (((1,), (1,)), ((), ())),
                                     preferred_element_type=F32)
                s_ref[n] = st
                x_ref[n] = jnp.max(st, axis=0, keepdims=True)

    def softmax_pv(j, s_ref, x_ref):
        start = pl.multiple_of(j * tq, tq)
        for hh in range(ATT_GROUP):
            vtb = vt_ref[0, hh * ATT_VD:(hh + 1) * ATT_VD, pl.ds(start, tq)]
            vtb = jnp.concatenate([vtb, ones_rows], axis=0)
            for n in (2 * hh, 2 * hh + 1):
                m_prev = m_ref[n]
                m_new = jnp.maximum(m_prev, x_ref[n])
                alpha = jnp.exp2(m_prev - m_new)
                pt = jnp.exp2(s_ref[n] - m_new)
                acc_ref[n] = alpha * acc_ref[n] + jnp.dot(vtb, pt.astype(BF16),
                                                          preferred_element_type=F32)
                m_ref[n] = m_new

    def softmax_pv_diagonal(j, s_ref):
        start = pl.multiple_of(j * tq, tq)
        hq = tq // 2
        key = lax.broadcasted_iota(jnp.int32, (hq, hq), 0)
        qry = lax.broadcasted_iota(jnp.int32, (hq, hq), 1)
        tri = key <= qry
        lo, hi = slice(0, hq), slice(hq, tq)
        for hh in range(ATT_GROUP):
            vtb = vt_ref[0, hh * ATT_VD:(hh + 1) * ATT_VD, pl.ds(start, tq)]
            vtb = jnp.concatenate([vtb, ones_rows], axis=0)
            for n in (2 * hh, 2 * hh + 1):
                s_a = jnp.where(tri, s_ref[n, lo, lo], NEG_BIG)
                s_b = s_ref[n, lo, hi]
                s_c = jnp.where(tri, s_ref[n, hi, hi], NEG_BIG)
                max_lo = jnp.max(s_a, axis=0, keepdims=True)
                max_hi = jnp.maximum(jnp.max(s_b, axis=0, keepdims=True),
                                     jnp.max(s_c, axis=0, keepdims=True))
                m_prev = m_ref[n]
                m_new = jnp.maximum(m_prev, jnp.concatenate([max_lo, max_hi], axis=1))
                alpha = jnp.exp2(m_prev - m_new)
                p_a = jnp.exp2(s_a - m_new[:, lo]).astype(BF16)
                p_bc = jnp.exp2(jnp.concatenate([s_b, s_c], axis=0) - m_new[:, hi]).astype(BF16)
                acc_ref[n, :, lo] = alpha[:, lo] * acc_ref[n, :, lo] + jnp.dot(
                    vtb[:, lo], p_a, preferred_element_type=F32)
                acc_ref[n, :, hi] = alpha[:, hi] * acc_ref[n, :, hi] + jnp.dot(
                    vtb, p_bc, preferred_element_type=F32)
                m_ref[n] = m_new

    def by_parity(j, fn):
        lax.cond(j % 2 == 0, lambda: fn(sa_ref, xa_ref, sb_ref, xb_ref),
                 lambda: fn(sb_ref, xb_ref, sa_ref, xa_ref))

    scores(0, sa_ref, xa_ref)

    def body(j, carry):
        def run(s_cur, x_cur, s_nxt, x_nxt):
            scores(j + 1, s_nxt, x_nxt)
            softmax_pv(j, s_cur, x_cur)
        by_parity(j, run)
        return carry

    lax.fori_loop(0, qi, body, 0)
    by_parity(qi, lambda s_cur, x_cur, s_nxt, x_nxt: softmax_pv_diagonal(qi, s_cur))

    lam_init = lamc_ref[0:1, 0:1]
    out_scale = lamc_ref[0:1, 1:2]
    lam = (jnp.exp(jnp.sum(lq1_ref[...] * lk1_ref[...], axis=-1, keepdims=True))
           - jnp.exp(jnp.sum(lq2_ref[...] * lk2_ref[...], axis=-1, keepdims=True))
           + lam_init)

    def normalized(n, weight):
        inv_den = weight / acc_ref[n, ATT_VD:ATT_VD + 1, :]
        return acc_ref[n, 0:ATT_VD, :] * inv_den

    for hh in range(ATT_GROUP):
        ot = normalized(2 * hh, 1.0) - normalized(2 * hh + 1, lam)
        inv = lax.rsqrt(jnp.mean(ot * ot, axis=0, keepdims=True) + EPS) * out_scale
        o_ref[0, :, hh * ATT_VD:(hh + 1) * ATT_VD] = (
            (ot * inv).T * sg_ref[...]).astype(BF16)


def _attn(q, k, vt, lq1, lk1, lq2, lk2, lamc, sg):
    b, s, _ = q.shape
    tq = TQ
    gw = ATT_GROUP * LANES
    streams = 2 * ATT_GROUP
    fixed = lambda bi, gi, qi: (0, 0)
    vec = pl.BlockSpec((1, ATT_HD), fixed)
    return pl.pallas_call(
        _attn_kernel,
        grid=(b, ATT_HEADS // ATT_GROUP, s // tq),
        in_specs=[
            pl.BlockSpec((1, tq, gw), lambda bi, gi, qi: (bi, qi, gi)),
            pl.BlockSpec((1, s, gw), lambda bi, gi, qi: (bi, 0, gi)),
            pl.BlockSpec((1, gw, s), lambda bi, gi, qi: (bi, gi, 0)),
            vec, vec, vec, vec,
            pl.BlockSpec((1, LANES), fixed),
            pl.BlockSpec((1, ATT_VD), fixed),
        ],
        out_specs=pl.BlockSpec((1, tq, gw), lambda bi, gi, qi: (bi, qi, gi)),
        out_shape=jax.ShapeDtypeStruct((b, s, ATT_OUT), BF16),
        scratch_shapes=[
            pltpu.VMEM((streams, tq, tq), F32),
            pltpu.VMEM((streams, tq, tq), F32),
            pltpu.VMEM((streams, 1, tq), F32),
            pltpu.VMEM((streams, 1, tq), F32),
            pltpu.VMEM((streams, 1, tq), F32),
            pltpu.VMEM((streams, ATT_VD + ATT_DEN_ROWS, tq), F32),
        ],
        compiler_params=_params("parallel", "parallel", "parallel"),
        name="diff_attn",
    )(q, k, vt, lq1, lk1, lq2, lk2, lamc, sg)


def _out_proj_kernel(x_ref, u_ref, o_ref, w_ref, g_ref, y_ref):
    m = jnp.dot(u_ref[...], w_ref[0:CONV_CH, :], preferred_element_type=F32)
    m = m + jnp.dot(o_ref[...], w_ref[CONV_CH:, :], preferred_element_type=F32)
    y_ref[...] = x_ref[...] + _rms(m, g_ref[...])


def _out_proj(x, u, o, w_all, layer, g):
    n = x.shape[0]
    tm = TM_PROJ
    row = lambda i: (i, 0)
    return pl.pallas_call(
        _out_proj_kernel,
        grid=(n // tm,),
        in_specs=[
            pl.BlockSpec((tm, D_MODEL), row),
            pl.BlockSpec((tm, CONV_CH), row),
            pl.BlockSpec((tm, ATT_OUT), row),
            _layer_resident(layer, (CONV_CH + ATT_OUT, D_MODEL)),
            _resident((1, D_MODEL)),
        ],
        out_specs=pl.BlockSpec((tm, D_MODEL), row),
        out_shape=jax.ShapeDtypeStruct((n, D_MODEL), F32),
        compiler_params=_params("parallel"),
        name="out_proj",
    )(x, u, o, w_all, g)


def _ffn_kernel(tiles_per_seq, x_ref, xprev_ref, gpre_ref, wup_ref, cw_ref, cb_ref, wd_ref,
                gpost_ref, y_ref, xs_ref, xp_ref, h_ref, ua_ref, ub_ref, act_ref, acc_ref):
    tm = x_ref.shape[0]
    ng = tm // SUBLANES
    nf = D_FF // FC
    first = (pl.program_id(0) % tiles_per_seq) == 0
    nlc = D_MODEL // LANES
    pitch = ng + SUBLANES
    for c in range(nlc):
        for s in range(SUBLANES):
            xs_ref[c, s * pitch:s * pitch + ng, :] = x_ref[s * ng:(s + 1) * ng,
                                                           c * LANES:(c + 1) * LANES]

    def strided_rows(g):
        return jnp.concatenate(
            [xs_ref[c, pl.ds(g, SUBLANES, stride=pitch), :] for c in range(nlc)], axis=1)

    for k in range(ng // 2):
        xp = jnp.concatenate([strided_rows(2 * k), strided_rows(2 * k + 1)], axis=0)
        xp_ref[2 * SUBLANES * k:2 * SUBLANES * (k + 1), :] = xp
        h_ref[2 * SUBLANES * k:2 * SUBLANES * (k + 1), :] = _rms(xp, gpre_ref[...]).astype(BF16)
    hp = _rms(xprev_ref[...], gpre_ref[...])
    h_ref[tm:, :] = jnp.where(first, 0.0, hp).astype(BF16)
    acc_ref[...] = jnp.zeros(acc_ref.shape, F32)
    sub0 = lax.broadcasted_iota(jnp.int32, (SUBLANES, FC), 0) == 0

    def cols(chunk):
        return pl.ds(pl.multiple_of(chunk * FC, FC), FC)

    def up_proj(j, u_ref):
        h = h_ref[...]
        u_ref[0] = jnp.dot(h, wup_ref[:, cols(j)], preferred_element_type=F32)
        u_ref[1] = jnp.dot(h, wup_ref[:, cols(nf + j)], preferred_element_type=F32)

    def process(j, u_ref, buf, slot):
        def conv(half, chunk):
            halo = u_ref[half, tm + FFN_HALO - SUBLANES:tm + FFN_HALO, :]
            edge = {}
            for m in range(1, FFN_K):
                tail = u_ref[half, tm - SUBLANES * m:tm - SUBLANES * (m - 1), :]
                edge[m] = jnp.where(sub0, pltpu.roll(halo, m, 0), pltpu.roll(tail, 1, 0))

            def lag(d):
                if d == 0:
                    return u_ref[half, 0:tm, :]
                groups = [edge[m] for m in range(d, 0, -1)]
                return jnp.concatenate(groups + [u_ref[half, 0:tm - SUBLANES * d, :]], axis=0)

            out = cb_ref[:, cols(chunk)]
            for t in range(FFN_K):
                out = out + cw_ref[t:t + 1, cols(chunk)] * lag(FFN_K - 1 - t)
            return out

        g = conv(0, j)
        val = conv(1, nf + j)
        e = jnp.exp2(g * (GELU_K1 + GELU_K3 * (g * g)))
        act_ref[buf, :, slot * FC:(slot + 1) * FC] = (
            g * val * (1.0 / (1.0 + e))).astype(BF16)

    def down_pair(p, buf):
        rows = pl.ds(pl.multiple_of(p * 2 * FC, 2 * FC), 2 * FC)
        acc_ref[...] += jnp.dot(act_ref[buf], wd_ref[rows, :], preferred_element_type=F32)

    def pair(p, buf, with_down):
        j = 2 * p
        up_proj(j + 1, ub_ref)
        if with_down:
            down_pair(p - 1, 1 - buf)
        process(j, ua_ref, buf, 0)
        up_proj(j + 2, ua_ref)
        process(j + 1, ub_ref, buf, 1)

    up_proj(0, ua_ref)
    pair(0, 0, False)

    def body(q, carry):
        pair(2 * q + 1, 1, True)
        pair(2 * q + 2, 0, True)
        return carry

    npairs = (nf - 1) // 2
    assert npairs % 2 == 1
    lax.fori_loop(0, (npairs - 1) // 2, body, 0)
    down_pair(npairs - 1, 0)
    process(nf - 1, ua_ref, 1, 0)
    acc = acc_ref[...] + jnp.dot(act_ref[1, :, 0:FC], wd_ref[(nf - 1) * FC:nf * FC, :],
                                 preferred_element_type=F32)
    yp = xp_ref[...] + _rms(acc, gpost_ref[...])
    for g in range(ng):
        for c in range(nlc):
            xs_ref[c, pl.ds(g, SUBLANES, stride=pitch), :] = (
                yp[SUBLANES * g:SUBLANES * (g + 1), c * LANES:(c + 1) * LANES])
    for c in range(nlc):
        for s in range(SUBLANES):
            y_ref[s * ng:(s + 1) * ng, c * LANES:(c + 1) * LANES] = (
                xs_ref[c, s * pitch:s * pitch + ng, :])


def _ffn(x, gpre, wup_all, cw, cb, wd_all, layer, gpost, seq):
    n = x.shape[0]
    tm = TM_FFN
    nf = D_FF // FC
    assert nf % 2 == 1
    tiles_per_seq = seq // tm
    halo_per_tile = tm // FFN_HALO
    row = lambda i: (i, 0)
    prev = lambda i: (jnp.maximum(i * halo_per_tile - 1, 0), 0)
    return pl.pallas_call(
        functools.partial(_ffn_kernel, tiles_per_seq),
        grid=(n // tm,),
        in_specs=[
            pl.BlockSpec((tm, D_MODEL), row),
            pl.BlockSpec((FFN_HALO, D_MODEL), prev),
            _resident((1, D_MODEL)),
            _layer_resident(layer, (D_MODEL, 2 * D_FF)),
            _resident((FFN_K, 2 * D_FF)),
            _resident((1, 2 * D_FF)),
            _layer_resident(layer, (D_FF, D_MODEL)),
            _resident((1, D_MODEL)),
        ],
        out_specs=pl.BlockSpec((tm, D_MODEL), row),
        out_shape=jax.ShapeDtypeStruct((n, D_MODEL), F32),
        scratch_shapes=[
            pltpu.VMEM((D_MODEL // LANES, tm + SUBLANES * SUBLANES, LANES), F32),
            pltpu.VMEM((tm, D_MODEL), F32),
            pltpu.VMEM((FFN_HALO + tm, D_MODEL), BF16),
            pltpu.VMEM((2, FFN_HALO + tm, FC), F32),
            pltpu.VMEM((2, FFN_HALO + tm, FC), F32),
            pltpu.VMEM((2, tm, 2 * FC), BF16),
            pltpu.VMEM((tm, D_MODEL), F32),
        ],
        compiler_params=_params("parallel"),
        name="ffn",
    )(x, x, gpre, wup_all, cw, cb, wd_all, gpost)


def _rope_tables(seq):
    half = ROT_DIM // 2
    pos = jnp.arange(seq, dtype=F32)
    inv_freq = ROPE_THETA ** (-jnp.arange(0, ROT_DIM, 2, dtype=F32) / ROT_DIM)
    ang = pos[:, None] * inv_freq[None, :]
    cos, sin = jnp.cos(ang), jnp.sin(ang)
    ones = jnp.ones((seq, ATT_HD - ROT_DIM), F32)
    zeros_rest = jnp.zeros((seq, ATT_HD - ROT_DIM), F32)
    zeros_half = jnp.zeros((seq, half), F32)
    c = jnp.concatenate([cos, cos, ones], axis=1)
    a = jnp.concatenate([zeros_half, sin, zeros_rest], axis=1)
    b = jnp.concatenate([-sin, zeros_half, zeros_rest], axis=1)
    tile2 = lambda t: jnp.concatenate([t, t], axis=1)
    return tile2(c), tile2(a), tile2(b)


def kernel(x, pre_mix_norm, w_in, conv_w, conv_b, conv_ln_g, conv_ln_b, lambda_q1, lambda_k1,
           lambda_q2, lambda_k2, subln_g, w_out, post_mix_norm, pre_ffn_norm, w_up, ffn_conv_w,
           ffn_conv_b, w_down, post_ffn_norm):
    b, s, d = x.shape
    n = b * s
    rc, ra, rb = _rope_tables(s)
    xf = x.reshape(n, d)
    row = lambda t: t.reshape(1, -1)
    w_in, w_out, w_up, w_down = (w.astype(BF16) for w in (w_in, w_out, w_up, w_down))
    for l in range(DEPTH):
        lam_init = 0.8 - 0.6 * math.exp(-0.3 * l)
        lamc = jnp.zeros((1, LANES), F32).at[0, 0].set(lam_init).at[0, 1].set(1.0 - lam_init)
        u, q, k, vt = _in_proj(xf, row(pre_mix_norm[l]), w_in, l, rc, ra, rb,
                               conv_w[l], row(conv_b[l]), row(conv_ln_g[l]), row(conv_ln_b[l]), s)
        o = _attn(q.reshape(b, s, QK_COLS), k.reshape(b, s, QK_COLS), vt,
                  row(lambda_q1[l]), row(lambda_k1[l]), row(lambda_q2[l]), row(lambda_k2[l]),
                  lamc, row(subln_g[l]))
        xf = _out_proj(xf, u, o.reshape(n, ATT_OUT), w_out, l, row(post_mix_norm[l]))
        xf = _ffn(xf, row(pre_ffn_norm[l]), w_up, ffn_conv_w[l], row(ffn_conv_b[l]), w_down, l,
                  row(post_ffn_norm[l]), s)
    return xf.reshape(b, s, d)
```

```python
import functools
import math

import jax
import jax.numpy as jnp
import numpy as np
from jax import lax
from jax.experimental import pallas as pl
from jax.experimental.pallas import tpu as pltpu

D_MODEL = 1024
DEPTH = 4
CONV_CH = 512
CONV_K = 31
ATT_HEADS = 4
ATT_HD = 64
ATT_VD = 2 * ATT_HD
ATT_OUT = ATT_HEADS * ATT_VD
QK_COLS = ATT_HEADS * 2 * ATT_HD
IN_COLS = 2 * CONV_CH + 2 * QK_COLS + ATT_OUT
ROT_DIM = ATT_HD // 4
ROPE_THETA = 500000.0
D_FF = 2816
FFN_K = 3
EPS = 1e-6

LANES = 128
SUBLANES = 8
VMEM_LIMIT = 56 * 1024 * 1024

TM_PROJ = 1024
CONV_HALO = 32
CONV_ROWS = 32
TQ = 512
ATT_GROUP = 4
ATT_DEN_ROWS = 16
TM_FFN = 512
FFN_HALO = 16
FC = 256
NEG_BIG = -1e30
GELU_K1 = -2.0 * math.log2(math.e) * math.sqrt(2.0 / math.pi)
GELU_K3 = GELU_K1 * 0.044715

F32 = jnp.float32
BF16 = jnp.bfloat16


def _params(*sem):
    return pltpu.CompilerParams(dimension_semantics=sem, vmem_limit_bytes=VMEM_LIMIT)


def _resident(shape):
    return pl.BlockSpec(shape, lambda *_: (0,) * len(shape), pipeline_mode=pl.Buffered(1))


def _layer_resident(layer, shape):
    return pl.BlockSpec((None,) + tuple(shape), lambda *_: (layer,) + (0,) * len(shape),
                        pipeline_mode=pl.Buffered(1))


def _rms(x, g):
    return x * lax.rsqrt(jnp.mean(x * x, axis=-1, keepdims=True) + EPS) * g


def _in_proj_kernel(tiles_per_seq, x_ref, xprev_ref, g_ref, w_ref, rc_ref, ra_ref, rb_ref,
                    cw_ref, cb_ref, lng_ref, lnb_ref, u_ref, q_ref, k_ref, vt_ref, sh_ref):
    first = (pl.program_id(0) % tiles_per_seq) == 0
    tm = x_ref.shape[0]
    h = _rms(x_ref[...], g_ref[...]).astype(BF16)
    hprev = _rms(xprev_ref[...], g_ref[...]).astype(BF16)
    hwin = jnp.concatenate([hprev, h], axis=0)

    def proj(lhs, c0):
        return jnp.dot(lhs, w_ref[:, c0:c0 + CONV_CH], preferred_element_type=F32)

    uwin = proj(hwin, 0) * jax.nn.sigmoid(proj(hwin, CONV_CH))
    sh_ref[0, 0:CONV_HALO, :] = jnp.where(first, 0.0, uwin[0:CONV_HALO])
    sh_ref[0, CONV_HALO:, :] = uwin[CONV_HALO:]
    win = sh_ref[0]
    nwin = CONV_HALO + tm
    for r in range(1, SUBLANES):
        sh_ref[r] = pltpu.roll(win, nwin - r, 0)
    base = CONV_HALO - (CONV_K - 1)
    groups = CONV_ROWS // SUBLANES

    def conv_rows(r0):
        acc = jnp.zeros((groups, SUBLANES, CONV_CH), F32)
        for k in range(CONV_K):
            phase = (base + k) % SUBLANES
            start = r0 + base + k - phase
            slab = sh_ref[phase, start:start + CONV_ROWS, :]
            acc = acc + cw_ref[k] * slab.reshape(groups, SUBLANES, CONV_CH)
        acc = acc.reshape(CONV_ROWS, CONV_CH) + cb_ref[...]
        mu = jnp.mean(acc, axis=-1, keepdims=True)
        d = acc - mu
        var = jnp.mean(d * d, axis=-1, keepdims=True)
        y = d * lax.rsqrt(var + EPS) * lng_ref[...] + lnb_ref[...]
        u_ref[r0:r0 + CONV_ROWS, :] = (y * jax.nn.sigmoid(y)).astype(BF16)

    rc, ra, rb = rc_ref[...], ra_ref[...], rb_ref[...]

    def rope(z):
        return (z * rc + pltpu.roll(z, ROT_DIM // 2, 1) * ra
                + pltpu.roll(z, LANES - ROT_DIM // 2, 1) * rb)

    for r0 in range(0, tm, CONV_ROWS):
        conv_rows(r0)

    zq = proj(h, 2 * CONV_CH)
    zk = proj(h, 2 * CONV_CH + QK_COLS)
    scale = math.log2(math.e) / math.sqrt(ATT_HD)
    for hd in range(ATT_HEADS):
        sl = slice(hd * LANES, (hd + 1) * LANES)
        q_ref[:, sl] = (rope(zq[:, sl]) * scale).astype(BF16)
        k_ref[:, sl] = rope(zk[:, sl]).astype(BF16)
    vt_ref[0] = proj(h, 2 * CONV_CH + 2 * QK_COLS).T.astype(BF16)


def _in_proj(x, g, w_all, layer, rc, ra, rb, cw, cb, lng, lnb, seq):
    n = x.shape[0]
    tm = TM_PROJ
    tiles_per_seq = seq // tm
    halo_per_tile = tm // CONV_HALO
    row = lambda i: (i, 0)
    prev = lambda i: (jnp.maximum(i * halo_per_tile - 1, 0), 0)
    pos = lambda i: (i % tiles_per_seq, 0)
    cw = jnp.broadcast_to(cw[:, None, :], (CONV_K, SUBLANES, CONV_CH))
    return pl.pallas_call(
        functools.partial(_in_proj_kernel, tiles_per_seq),
        grid=(n // tm,),
        in_specs=[
            pl.BlockSpec((tm, D_MODEL), row),
            pl.BlockSpec((CONV_HALO, D_MODEL), prev),
            _resident((1, D_MODEL)),
            _layer_resident(layer, (D_MODEL, IN_COLS)),
            pl.BlockSpec((tm, LANES), pos),
            pl.BlockSpec((tm, LANES), pos),
            pl.BlockSpec((tm, LANES), pos),
            _resident((CONV_K, SUBLANES, CONV_CH)),
            _resident((1, CONV_CH)),
            _resident((1, CONV_CH)),
            _resident((1, CONV_CH)),
        ],
        out_specs=[
            pl.BlockSpec((tm, CONV_CH), row),
            pl.BlockSpec((tm, QK_COLS), row),
            pl.BlockSpec((tm, QK_COLS), row),
            pl.BlockSpec((1, ATT_OUT, tm), lambda i: (i // tiles_per_seq, 0, i % tiles_per_seq)),
        ],
        out_shape=[
            jax.ShapeDtypeStruct((n, CONV_CH), BF16),
            jax.ShapeDtypeStruct((n, QK_COLS), BF16),
            jax.ShapeDtypeStruct((n, QK_COLS), BF16),
            jax.ShapeDtypeStruct((n // seq, ATT_OUT, seq), BF16),
        ],
        scratch_shapes=[pltpu.VMEM((SUBLANES, CONV_HALO + tm, CONV_CH), F32)],
        compiler_params=_params("parallel"),
        name="in_proj",
    )(x, x, g, w_all, rc, ra, rb, cw, cb, lng, lnb)


def _attn_kernel(q_ref, k_ref, vt_ref, lq1_ref, lk1_ref, lq2_ref, lk2_ref, lamc_ref, sg_ref,
                 o_ref, sa_ref, sb_ref, xa_ref, xb_ref, m_ref, acc_ref):
    qi = pl.program_id(2)
    tq = q_ref.shape[1]
    lane = lax.broadcasted_iota(jnp.int32, (tq, LANES), 1)
    qs = []
    for hh in range(ATT_GROUP):
        q = q_ref[0, :, hh * LANES:(hh + 1) * LANES]
        zero = jnp.zeros_like(q)
        qs += [jnp.where(lane < ATT_HD, q, zero), jnp.where(lane >= ATT_HD, q, zero)]

    m_ref[...] = jnp.full(m_ref.shape, NEG_BIG, F32)
    acc_ref[...] = jnp.zeros(acc_ref.shape, F32)
    ones_rows = jnp.ones((ATT_DEN_ROWS, tq), BF16)

    def scores(j, s_ref, x_ref):
        start = pl.multiple_of(j * tq, tq)
        for hh in range(ATT_GROUP):
            kb = k_ref[0, pl.ds(start, tq), hh * LANES:(hh + 1) * LANES]
            for n in (2 * hh, 2 * hh + 1):
                st = lax.dot_general(kb, qs[n], (((1,), (1,)), ((), ())),
                                     preferred_element_type=F32)
                s_ref[n] = st
                x_ref[n] = jnp.max(st, axis=0, keepdims=True)

    def softmax_pv(j, s_ref, x_ref):
        start = pl.multiple_of(j * tq, tq)
        for hh in range(ATT_GROUP):
            vtb = vt_ref[0, hh * ATT_VD:(hh + 1) * ATT_VD, pl.ds(start, tq)]
            vtb = jnp.concatenate([vtb, ones_rows], axis=0)
            for n in (2 * hh, 2 * hh + 1):
                m_prev = m_ref[n]
                m_new = jnp.maximum(m_prev, x_ref[n])
                alpha = jnp.exp2(m_prev - m_new)
                pt = jnp.exp2(s_ref[n] - m_new)
                acc_ref[n] = alpha * acc_ref[n] + jnp.dot(vtb, pt.astype(BF16),
                                                          preferred_element_type=F32)
                m_ref[n] = m_new

    def softmax_pv_diagonal(j, s_ref):
        start = pl.multiple_of(j * tq, tq)
        hq = tq // 2
        key = lax.broadcasted_iota(jnp.int32, (hq, hq), 0)
        qry = lax.broadcasted_iota(jnp.int32, (hq, hq), 1)
        tri = key <= qry
        lo, hi = slice(0, hq), slice(hq, tq)
        for hh in range(ATT_GROUP):
            vtb = vt_ref[0, hh * ATT_VD:(hh + 1) * ATT_VD, pl.ds(start, tq)]
            vtb = jnp.concatenate([vtb, ones_rows], axis=0)
            for n in (2 * hh, 2 * hh + 1):
                s_a = jnp.where(tri, s_ref[n, lo, lo], NEG_BIG)
                s_b = s_ref[n, lo, hi]
                s_c = jnp.where(tri, s_ref[n, hi, hi], NEG_BIG)
                max_lo = jnp.max(s_a, axis=0, keepdims=True)
                max_hi = jnp.maximum(jnp.max(s_b, axis=0, keepdims=True),
                                     jnp.max(s_c, axis=0, keepdims=True))
                m_prev = m_ref[n]
                m_new = jnp.maximum(m_prev, jnp.concatenate([max_lo, max_hi], axis=1))
                alpha = jnp.exp2(m_prev - m_new)
                p_a = jnp.exp2(s_a - m_new[:, lo]).astype(BF16)
                p_bc = jnp.exp2(jnp.concatenate([s_b, s_c], axis=0) - m_new[:, hi]).astype(BF16)
                acc_ref[n, :, lo] = alpha[:, lo] * acc_ref[n, :, lo] + jnp.dot(
                    vtb[:, lo], p_a, preferred_element_type=F32)
                acc_ref[n, :, hi] = alpha[:, hi] * acc_ref[n, :, hi] + jnp.dot(
                    vtb, p_bc, preferred_element_type=F32)
                m_ref[n] = m_new

    def by_parity(j, fn):
        lax.cond(j % 2 == 0, lambda: fn(sa_ref, xa_ref, sb_ref, xb_ref),
                 lambda: fn(sb_ref, xb_ref, sa_ref, xa_ref))

    scores(0, sa_ref, xa_ref)

    def body(j, carry):
        def run(s_cur, x_cur, s_nxt, x_nxt):
            scores(j + 1, s_nxt, x_nxt)
            softmax_pv(j, s_cur, x_cur)
        by_parity(j, run)
        return carry

    lax.fori_loop(0, qi, body, 0)
    by_parity(qi, lambda s_cur, x_cur, s_nxt, x_nxt: softmax_pv_diagonal(qi, s_cur))

    lam_init = lamc_ref[0:1, 0:1]
    out_scale = lamc_ref[0:1, 1:2]
    lam = (jnp.exp(jnp.sum(lq1_ref[...] * lk1_ref[...], axis=-1, keepdims=True))
           - jnp.exp(jnp.sum(lq2_ref[...] * lk2_ref[...], axis=-1, keepdims=True))
           + lam_init)

    def normalized(n, weight):
        inv_den = weight / acc_ref[n, ATT_VD:ATT_VD + 1, :]
        return acc_ref[n, 0:ATT_VD, :] * inv_den

    for hh in range(ATT_GROUP):
        ot = normalized(2 * hh, 1.0) - normalized(2 * hh + 1, lam)
        inv = lax.rsqrt(jnp.mean(ot * ot, axis=0, keepdims=True) + EPS) * out_scale
        o_ref[0, :, hh * ATT_VD:(hh + 1) * ATT_VD] = (
            (ot * inv).T * sg_ref[...]).astype(BF16)


def _attn(q, k, vt, lq1, lk1, lq2, lk2, lamc, sg):
    b, s, _ = q.shape
    tq = TQ
    gw = ATT_GROUP * LANES
    streams = 2 * ATT_GROUP
    fixed = lambda bi, gi, qi: (0, 0)
    vec = pl.BlockSpec((1, ATT_HD), fixed)
    return pl.pallas_call(
        _attn_kernel,
        grid=(b, ATT_HEADS // ATT_GROUP, s // tq),
        in_specs=[
            pl.BlockSpec((1, tq, gw), lambda bi, gi, qi: (bi, qi, gi)),
            pl.BlockSpec((1, s, gw), lambda bi, gi, qi: (bi, 0, gi)),
            pl.BlockSpec((1, gw, s), lambda bi, gi, qi: (bi, gi, 0)),
            vec, vec, vec, vec,
            pl.BlockSpec((1, LANES), fixed),
            pl.BlockSpec((1, ATT_VD), fixed),
        ],
        out_specs=pl.BlockSpec((1, tq, gw), lambda bi, gi, qi: (bi, qi, gi)),
        out_shape=jax.ShapeDtypeStruct((b, s, ATT_OUT), BF16),
        scratch_shapes=[
            pltpu.VMEM((streams, tq, tq), F32),
            pltpu.VMEM((streams, tq, tq), F32),
            pltpu.VMEM((streams, 1, tq), F32),
            pltpu.VMEM((streams, 1, tq), F32),
            pltpu.VMEM((streams, 1, tq), F32),
            pltpu.VMEM((streams, ATT_VD + ATT_DEN_ROWS, tq), F32),
        ],
        compiler_params=_params("parallel", "parallel", "parallel"),
        name="diff_attn",
    )(q, k, vt, lq1, lk1, lq2, lk2, lamc, sg)


def _out_proj_kernel(x_ref, u_ref, o_ref, w_ref, g_ref, y_ref):
    m = jnp.dot(u_ref[...], w_ref[0:CONV_CH, :], preferred_element_type=F32)
    m = m + jnp.dot(o_ref[...], w_ref[CONV_CH:, :], preferred_element_type=F32)
    y_ref[...] = x_ref[...] + _rms(m, g_ref[...])


def _out_proj(x, u, o, w_all, layer, g):
    n = x.shape[0]
    tm = TM_PROJ
    row = lambda i: (i, 0)
    return pl.pallas_call(
        _out_proj_kernel,
        grid=(n // tm,),
        in_specs=[
            pl.BlockSpec((tm, D_MODEL), row),
            pl.BlockSpec((tm, CONV_CH), row),
            pl.BlockSpec((tm, ATT_OUT), row),
            _layer_resident(layer, (CONV_CH + ATT_OUT, D_MODEL)),
            _resident((1, D_MODEL)),
        ],
        out_specs=pl.BlockSpec((tm, D_MODEL), row),
        out_shape=jax.ShapeDtypeStruct((n, D_MODEL), F32),
        compiler_params=_params("parallel"),
        name="out_proj",
    )(x, u, o, w_all, g)


def _ffn_kernel(tiles_per_seq, x_ref, xprev_ref, gpre_ref, wup_ref, cw_ref, cb_ref, wd_ref,
                gpost_ref, y_ref, xs_ref, xp_ref, h_ref, ua_ref, ub_ref, act_ref, acc_ref):
    tm = x_ref.shape[0]
    ng = tm // SUBLANES
    nf = D_FF // FC
    first = (pl.program_id(0) % tiles_per_seq) == 0
    nlc = D_MODEL // LANES
    pitch = ng + SUBLANES
    for c in range(nlc):
        for s in range(SUBLANES):
            xs_ref[c, s * pitch:s * pitch + ng, :] = x_ref[s * ng:(s + 1) * ng,
                                                           c * LANES:(c + 1) * LANES]

    def strided_rows(g):
        return jnp.concatenate(
            [xs_ref[c, pl.ds(g, SUBLANES, stride=pitch), :] for c in range(nlc)], axis=1)

    for k in range(ng // 2):
        xp = jnp.concatenate([strided_rows(2 * k), strided_rows(2 * k + 1)], axis=0)
        xp_ref[2 * SUBLANES * k:2 * SUBLANES * (k + 1), :] = xp
        h_ref[2 * SUBLANES * k:2 * SUBLANES * (k + 1), :] = _rms(xp, gpre_ref[...]).astype(BF16)
    hp = _rms(xprev_ref[...], gpre_ref[...])
    h_ref[tm:, :] = jnp.where(first, 0.0, hp).astype(BF16)
    acc_ref[...] = jnp.zeros(acc_ref.shape, F32)
    sub0 = lax.broadcasted_iota(jnp.int32, (SUBLANES, FC), 0) == 0

    def cols(chunk):
        return pl.ds(pl.multiple_of(chunk * FC, FC), FC)

    def up_proj(j, u_ref):
        h = h_ref[...]
        u_ref[0] = jnp.dot(h, wup_ref[:, cols(j)], preferred_element_type=F32)
        u_ref[1] = jnp.dot(h, wup_ref[:, cols(nf + j)], preferred_element_type=F32)

    def process(j, u_ref, buf, slot):
        def conv(half, chunk):
            halo = u_ref[half, tm + FFN_HALO - SUBLANES:tm + FFN_HALO, :]
            edge = {}
            for m in range(1, FFN_K):
                tail = u_ref[half, tm - SUBLANES * m:tm - SUBLANES * (m - 1), :]
                edge[m] = jnp.where(sub0, pltpu.roll(halo, m, 0), pltpu.roll(tail, 1, 0))

            def lag(d):
                if d == 0:
                    return u_ref[half, 0:tm, :]
                groups = [edge[m] for m in range(d, 0, -1)]
                return jnp.concatenate(groups + [u_ref[half, 0:tm - SUBLANES * d, :]], axis=0)

            out = cb_ref[:, cols(chunk)]
            for t in range(FFN_K):
                out = out + cw_ref[t:t + 1, cols(chunk)] * lag(FFN_K - 1 - t)
            return out

        g = conv(0, j)
        val = conv(1, nf + j)
        e = jnp.exp2(g * (GELU_K1 + GELU_K3 * (g * g)))
        act_ref[buf, :, slot * FC:(slot + 1) * FC] = (
            g * val * (1.0 / (1.0 + e))).astype(BF16)

    def down_pair(p, buf):
        rows = pl.ds(pl.multiple_of(p * 2 * FC, 2 * FC), 2 * FC)
        acc_ref[...] += jnp.dot(act_ref[buf], wd_ref[rows, :], preferred_element_type=F32)

    def pair(p, buf, with_down):
        j = 2 * p
        up_proj(j + 1, ub_ref)
        if with_down:
            down_pair(p - 1, 1 - buf)
        process(j, ua_ref, buf, 0)
        up_proj(j + 2, ua_ref)
        process(j + 1, ub_ref, buf, 1)

    up_proj(0, ua_ref)
    pair(0, 0, False)

    def body(q, carry):
        pair(2 * q + 1, 1, True)
        pair(2 * q + 2, 0, True)
        return carry

    npairs = (nf - 1) // 2
    assert npairs % 2 == 1
    lax.fori_loop(0, (npairs - 1) // 2, body, 0)
    down_pair(npairs - 1, 0)
    process(nf - 1, ua_ref, 1, 0)
    acc = acc_ref[...] + jnp.dot(act_ref[1, :, 0:FC], wd_ref[(nf - 1) * FC:nf * FC, :],
                                 preferred_element_type=F32)
    yp = xp_ref[...] + _rms(acc, gpost_ref[...])
    for g in range(ng):
        for c in range(nlc):
            xs_ref[c, pl.ds(g, SUBLANES, stride=pitch), :] = (
                yp[SUBLANES * g:SUBLANES * (g + 1), c * LANES:(c + 1) * LANES])
    for c in range(nlc):
        for s in range(SUBLANES):
            y_ref[s * ng:(s + 1) * ng, c * LANES:(c + 1) * LANES] = (
                xs_ref[c, s * pitch:s * pitch + ng, :])


def _ffn(x, gpre, wup_all, cw, cb, wd_all, layer, gpost, seq):
    n = x.shape[0]
    tm = TM_FFN
    nf = D_FF // FC
    assert nf % 2 == 1
    tiles_per_seq = seq // tm
    halo_per_tile = tm // FFN_HALO
    row = lambda i: (i, 0)
    prev = lambda i: (jnp.maximum(i * halo_per_tile - 1, 0), 0)
    return pl.pallas_call(
        functools.partial(_ffn_kernel, tiles_per_seq),
        grid=(n // tm,),
        in_specs=[
            pl.BlockSpec((tm, D_MODEL), row),
            pl.BlockSpec((FFN_HALO, D_MODEL), prev),
            _resident((1, D_MODEL)),
            _layer_resident(layer, (D_MODEL, 2 * D_FF)),
            _resident((FFN_K, 2 * D_FF)),
            _resident((1, 2 * D_FF)),
            _layer_resident(layer, (D_FF, D_MODEL)),
            _resident((1, D_MODEL)),
        ],
        out_specs=pl.BlockSpec((tm, D_MODEL), row),
        out_shape=jax.ShapeDtypeStruct((n, D_MODEL), F32),
        scratch_shapes=[
            pltpu.VMEM((D_MODEL // LANES, tm + SUBLANES * SUBLANES, LANES), F32),
            pltpu.VMEM((tm, D_MODEL), F32),
            pltpu.VMEM((FFN_HALO + tm, D_MODEL), BF16),
            pltpu.VMEM((2, FFN_HALO + tm, FC), F32),
            pltpu.VMEM((2, FFN_HALO + tm, FC), F32),
            pltpu.VMEM((2, tm, 2 * FC), BF16),
            pltpu.VMEM((tm, D_MODEL), F32),
        ],
        compiler_params=_params("parallel"),
        name="ffn",
    )(x, x, gpre, wup_all, cw, cb, wd_all, gpost)


def _rope_tables(seq):
    half = ROT_DIM // 2
    f32 = np.float32
    pos = np.arange(seq, dtype=f32)
    inv_freq = f32(ROPE_THETA) ** (-np.arange(0, ROT_DIM, 2, dtype=f32) / f32(ROT_DIM))
    ang = pos[:, None] * inv_freq[None, :]
    cos, sin = np.cos(ang).astype(f32), np.sin(ang).astype(f32)
    ones = np.ones((seq, ATT_HD - ROT_DIM), f32)
    zeros_rest = np.zeros((seq, ATT_HD - ROT_DIM), f32)
    zeros_half = np.zeros((seq, half), f32)
    c = np.concatenate([cos, cos, ones], axis=1)
    a = np.concatenate([zeros_half, sin, zeros_rest], axis=1)
    b = np.concatenate([-sin, zeros_half, zeros_rest], axis=1)
    tile2 = lambda t: jnp.asarray(np.concatenate([t, t], axis=1))
    return tile2(c), tile2(a), tile2(b)


def kernel(x, pre_mix_norm, w_in, conv_w, conv_b, conv_ln_g, conv_ln_b, lambda_q1, lambda_k1,
           lambda_q2, lambda_k2, subln_g, w_out, post_mix_norm, pre_ffn_norm, w_up, ffn_conv_w,
           ffn_conv_b, w_down, post_ffn_norm):
    b, s, d = x.shape
    n = b * s
    rc, ra, rb = _rope_tables(s)
    xf = x.reshape(n, d)
    row = lambda t: t.reshape(1, -1)
    w_in, w_out, w_up, w_down = (w.astype(BF16) for w in (w_in, w_out, w_up, w_down))
    for l in range(DEPTH):
        lam_init = 0.8 - 0.6 * math.exp(-0.3 * l)
        lamc_host = np.zeros((1, LANES), np.float32)
        lamc_host[0, 0], lamc_host[0, 1] = lam_init, 1.0 - lam_init
        lamc = jnp.asarray(lamc_host)
        u, q, k, vt = _in_proj(xf, row(pre_mix_norm[l]), w_in, l, rc, ra, rb,
                               conv_w[l], row(conv_b[l]), row(conv_ln_g[l]), row(conv_ln_b[l]), s)
        o = _attn(q.reshape(b, s, QK_COLS), k.reshape(b, s, QK_COLS), vt,
                  row(lambda_q1[l]), row(lambda_k1[l]), row(lambda_q2[l]), row(lambda_k2[l]),
                  lamc, row(subln_g[l]))
        xf = _out_proj(xf, u, o.reshape(n, ATT_OUT), w_out, l, row(post_mix_norm[l]))
        xf = _ffn(xf, row(pre_ffn_norm[l]), w_up, ffn_conv_w[l], row(ffn_conv_b[l]), w_down, l,
                  row(post_ffn_norm[l]), s)
    return xf.reshape(b, s, d)
```

```python
import functools
import math

import jax
import jax.numpy as jnp
from jax import lax
from jax.experimental import pallas as pl
from jax.experimental.pallas import tpu as pltpu

D_MODEL = 1024
DEPTH = 4
CONV_CH = 512
CONV_K = 31
ATT_HEADS = 4
ATT_HD = 64
ATT_VD = 2 * ATT_HD
ATT_OUT = ATT_HEADS * ATT_VD
QK_COLS = ATT_HEADS * 2 * ATT_HD
IN_COLS = 2 * CONV_CH + 2 * QK_COLS + ATT_OUT
ROT_DIM = ATT_HD // 4
ROPE_THETA = 500000.0
D_FF = 2816
FFN_K = 3
EPS = 1e-6

LANES = 128
SUBLANES = 8
VMEM_LIMIT = 56 * 1024 * 1024

TM_PROJ = 1024
CONV_HALO = 32
CONV_ROWS = 32
TQ = 512
ATT_GROUP = 2
ATT_DEN_ROWS = 16
TM_FFN = 512
FFN_HALO = 16
FC = 256
NEG_BIG = -1e30
GELU_K1 = -2.0 * math.log2(math.e) * math.sqrt(2.0 / math.pi)
GELU_K3 = GELU_K1 * 0.044715

F32 = jnp.float32
BF16 = jnp.bfloat16


def _params(*sem):
    return pltpu.CompilerParams(dimension_semantics=sem, vmem_limit_bytes=VMEM_LIMIT)


def _resident(shape):
    return pl.BlockSpec(shape, lambda *_: (0,) * len(shape), pipeline_mode=pl.Buffered(1))


def _layer_resident(layer, shape):
    return pl.BlockSpec((None,) + tuple(shape), lambda *_: (layer,) + (0,) * len(shape),
                        pipeline_mode=pl.Buffered(1))


def _rms(x, g):
    return x * lax.rsqrt(jnp.mean(x * x, axis=-1, keepdims=True) + EPS) * g


def _in_proj_kernel(tiles_per_seq, x_ref, xprev_ref, g_ref, w_ref, rc_ref, ra_ref, rb_ref,
                    cw_ref, cb_ref, lng_ref, lnb_ref, u_ref, q_ref, k_ref, vt_ref, sh_ref):
    first = (pl.program_id(0) % tiles_per_seq) == 0
    tm = x_ref.shape[0]
    h = _rms(x_ref[...], g_ref[...]).astype(BF16)
    hprev = _rms(xprev_ref[...], g_ref[...]).astype(BF16)
    hwin = jnp.concatenate([hprev, h], axis=0)

    def proj(lhs, c0):
        return jnp.dot(lhs, w_ref[:, c0:c0 + CONV_CH], preferred_element_type=F32)

    uwin = proj(hwin, 0) * jax.nn.sigmoid(proj(hwin, CONV_CH))
    sh_ref[0, 0:CONV_HALO, :] = jnp.where(first, 0.0, uwin[0:CONV_HALO])
    sh_ref[0, CONV_HALO:, :] = uwin[CONV_HALO:]
    win = sh_ref[0]
    nwin = CONV_HALO + tm
    for r in range(1, SUBLANES):
        sh_ref[r] = pltpu.roll(win, nwin - r, 0)
    base = CONV_HALO - (CONV_K - 1)
    groups = CONV_ROWS // SUBLANES

    def conv_rows(r0):
        acc = jnp.zeros((groups, SUBLANES, CONV_CH), F32)
        for k in range(CONV_K):
            phase = (base + k) % SUBLANES
            start = r0 + base + k - phase
            slab = sh_ref[phase, start:start + CONV_ROWS, :]
            acc = acc + cw_ref[k] * slab.reshape(groups, SUBLANES, CONV_CH)
        acc = acc.reshape(CONV_ROWS, CONV_CH) + cb_ref[...]
        mu = jnp.mean(acc, axis=-1, keepdims=True)
        d = acc - mu
        var = jnp.mean(d * d, axis=-1, keepdims=True)
        y = d * lax.rsqrt(var + EPS) * lng_ref[...] + lnb_ref[...]
        u_ref[r0:r0 + CONV_ROWS, :] = (y * jax.nn.sigmoid(y)).astype(BF16)

    rc, ra, rb = rc_ref[...], ra_ref[...], rb_ref[...]

    def rope(z):
        return (z * rc + pltpu.roll(z, ROT_DIM // 2, 1) * ra
                + pltpu.roll(z, LANES - ROT_DIM // 2, 1) * rb)

    for r0 in range(0, tm, CONV_ROWS):
        conv_rows(r0)

    zq = proj(h, 2 * CONV_CH)
    zk = proj(h, 2 * CONV_CH + QK_COLS)
    scale = math.log2(math.e) / math.sqrt(ATT_HD)
    for hd in range(ATT_HEADS):
        sl = slice(hd * LANES, (hd + 1) * LANES)
        q_ref[:, sl] = (rope(zq[:, sl]) * scale).astype(BF16)
        k_ref[:, sl] = rope(zk[:, sl]).astype(BF16)
    vt_ref[0] = proj(h, 2 * CONV_CH + 2 * QK_COLS).T.astype(BF16)


def _in_proj(x, g, w_all, layer, rc, ra, rb, cw, cb, lng, lnb, seq):
    n = x.shape[0]
    tm = TM_PROJ
    tiles_per_seq = seq // tm
    halo_per_tile = tm // CONV_HALO
    row = lambda i: (i, 0)
    prev = lambda i: (jnp.maximum(i * halo_per_tile - 1, 0), 0)
    pos = lambda i: (i % tiles_per_seq, 0)
    cw = jnp.broadcast_to(cw[:, None, :], (CONV_K, SUBLANES, CONV_CH))
    return pl.pallas_call(
        functools.partial(_in_proj_kernel, tiles_per_seq),
        grid=(n // tm,),
        in_specs=[
            pl.BlockSpec((tm, D_MODEL), row),
            pl.BlockSpec((CONV_HALO, D_MODEL), prev),
            _resident((1, D_MODEL)),
            _layer_resident(layer, (D_MODEL, IN_COLS)),
            pl.BlockSpec((tm, LANES), pos),
            pl.BlockSpec((tm, LANES), pos),
            pl.BlockSpec((tm, LANES), pos),
            _resident((CONV_K, SUBLANES, CONV_CH)),
            _resident((1, CONV_CH)),
            _resident((1, CONV_CH)),
            _resident((1, CONV_CH)),
        ],
        out_specs=[
            pl.BlockSpec((tm, CONV_CH), row),
            pl.BlockSpec((tm, QK_COLS), row),
            pl.BlockSpec((tm, QK_COLS), row),
            pl.BlockSpec((1, ATT_OUT, tm), lambda i: (i // tiles_per_seq, 0, i % tiles_per_seq)),
        ],
        out_shape=[
            jax.ShapeDtypeStruct((n, CONV_CH), BF16),
            jax.ShapeDtypeStruct((n, QK_COLS), BF16),
            jax.ShapeDtypeStruct((n, QK_COLS), BF16),
            jax.ShapeDtypeStruct((n // seq, ATT_OUT, seq), BF16),
        ],
        scratch_shapes=[pltpu.VMEM((SUBLANES, CONV_HALO + tm, CONV_CH), F32)],
        compiler_params=_params("parallel"),
        name="in_proj",
    )(x, x, g, w_all, rc, ra, rb, cw, cb, lng, lnb)


def _attn_kernel(q_ref, k_ref, vt_ref, lq1_ref, lk1_ref, lq2_ref, lk2_ref, lamc_ref, sg_ref,
                 o_ref, sa_ref, sb_ref, xa_ref, xb_ref, m_ref, acc_ref):
    qi = pl.program_id(2)
    tq = q_ref.shape[1]
    lane = lax.broadcasted_iota(jnp.int32, (tq, LANES), 1)
    qs = []
    for hh in range(ATT_GROUP):
        q = q_ref[0, :, hh * LANES:(hh + 1) * LANES]
        zero = jnp.zeros_like(q)
        qs += [jnp.where(lane < ATT_HD, q, zero), jnp.where(lane >= ATT_HD, q, zero)]

    m_ref[...] = jnp.full(m_ref.shape, NEG_BIG, F32)
    acc_ref[...] = jnp.zeros(acc_ref.shape, F32)
    ones_rows = jnp.ones((ATT_DEN_ROWS, tq), BF16)

    def scores(j, s_ref, x_ref):
        start = pl.multiple_of(j * tq, tq)
        for hh in range(ATT_GROUP):
            kb = k_ref[0, pl.ds(start, tq), hh * LANES:(hh + 1) * LANES]
            for n in (2 * hh, 2 * hh + 1):
                st = lax.dot_general(kb, qs[n], (((1,), (1,)), ((), ())),
                                     preferred_element_type=F32)
                s_ref[n] = st
                x_ref[n] = jnp.max(st, axis=0, keepdims=True)

    def softmax_pv(j, s_ref, x_ref):
        start = pl.multiple_of(j * tq, tq)
        for hh in range(ATT_GROUP):
            vtb = vt_ref[0, hh * ATT_VD:(hh + 1) * ATT_VD, pl.ds(start, tq)]
            vtb = jnp.concatenate([vtb, ones_rows], axis=0)
            for n in (2 * hh, 2 * hh + 1):
                m_prev = m_ref[n]
                m_new = jnp.maximum(m_prev, x_ref[n])
                alpha = jnp.exp2(m_prev - m_new)
                pt = jnp.exp2(s_ref[n] - m_new)
                acc_ref[n] = alpha * acc_ref[n] + jnp.dot(vtb, pt.astype(BF16),
                                                          preferred_element_type=F32)
                m_ref[n] = m_new

    def softmax_pv_diagonal(j, s_ref):
        start = pl.multiple_of(j * tq, tq)
        hq = tq // 2
        key = lax.broadcasted_iota(jnp.int32, (hq, hq), 0)
        qry = lax.broadcasted_iota(jnp.int32, (hq, hq), 1)
        tri = key <= qry
        lo, hi = slice(0, hq), slice(hq, tq)
        for hh in range(ATT_GROUP):
            vtb = vt_ref[0, hh * ATT_VD:(hh + 1) * ATT_VD, pl.ds(start, tq)]
            vtb = jnp.concatenate([vtb, ones_rows], axis=0)
            for n in (2 * hh, 2 * hh + 1):
                s_a = jnp.where(tri, s_ref[n, lo, lo], NEG_BIG)
                s_b = s_ref[n, lo, hi]
                s_c = jnp.where(tri, s_ref[n, hi, hi], NEG_BIG)
                max_lo = jnp.max(s_a, axis=0, keepdims=True)
                max_hi = jnp.maximum(jnp.max(s_b, axis=0, keepdims=True),
                                     jnp.max(s_c, axis=0, keepdims=True))
                m_prev = m_ref[n]
                m_new = jnp.maximum(m_prev, jnp.concatenate([max_lo, max_hi], axis=1))
                alpha = jnp.exp2(m_prev - m_new)
                p_a = jnp.exp2(s_a - m_new[:, lo]).astype(BF16)
                p_bc = jnp.exp2(jnp.concatenate([s_b, s_c], axis=0) - m_new[:, hi]).astype(BF16)
                acc_ref[n, :, lo] = alpha[:, lo] * acc_ref[n, :, lo] + jnp.dot(
                    vtb[:, lo], p_a, preferred_element_type=F32)
                acc_ref[n, :, hi] = alpha[:, hi] * acc_ref[n, :, hi] + jnp.dot(
                    vtb, p_bc, preferred_element_type=F32)
                m_ref[n] = m_new

    def by_parity(j, fn):
        lax.cond(j % 2 == 0, lambda: fn(sa_ref, xa_ref, sb_ref, xb_ref),
                 lambda: fn(sb_ref, xb_ref, sa_ref, xa_ref))

    scores(0, sa_ref, xa_ref)

    def body(j, carry):
        def run(s_cur, x_cur, s_nxt, x_nxt):
            scores(j + 1, s_nxt, x_nxt)
            softmax_pv(j, s_cur, x_cur)
        by_parity(j, run)
        return carry

    lax.fori_loop(0, qi, body, 0)
    by_parity(qi, lambda s_cur, x_cur, s_nxt, x_nxt: softmax_pv_diagonal(qi, s_cur))

    lam_init = lamc_ref[0:1, 0:1]
    out_scale = lamc_ref[0:1, 1:2]
    lam = (jnp.exp(jnp.sum(lq1_ref[...] * lk1_ref[...], axis=-1, keepdims=True))
           - jnp.exp(jnp.sum(lq2_ref[...] * lk2_ref[...], axis=-1, keepdims=True))
           + lam_init)

    def normalized(n, weight):
        inv_den = weight / acc_ref[n, ATT_VD:ATT_VD + 1, :]
        return acc_ref[n, 0:ATT_VD, :] * inv_den

    for hh in range(ATT_GROUP):
        ot = normalized(2 * hh, 1.0) - normalized(2 * hh + 1, lam)
        inv = lax.rsqrt(jnp.mean(ot * ot, axis=0, keepdims=True) + EPS) * out_scale
        o_ref[0, :, hh * ATT_VD:(hh + 1) * ATT_VD] = (
            (ot * inv).T * sg_ref[...]).astype(BF16)


def _attn(q, k, vt, lq1, lk1, lq2, lk2, lamc, sg):
    b, s, _ = q.shape
    tq = TQ
    gw = ATT_GROUP * LANES
    streams = 2 * ATT_GROUP
    fixed = lambda bi, gi, qi: (0, 0)
    vec = pl.BlockSpec((1, ATT_HD), fixed)
    return pl.pallas_call(
        _attn_kernel,
        grid=(b, ATT_HEADS // ATT_GROUP, s // tq),
        in_specs=[
            pl.BlockSpec((1, tq, gw), lambda bi, gi, qi: (bi, qi, gi)),
            pl.BlockSpec((1, s, gw), lambda bi, gi, qi: (bi, 0, gi)),
            pl.BlockSpec((1, gw, s), lambda bi, gi, qi: (bi, gi, 0)),
            vec, vec, vec, vec,
            pl.BlockSpec((1, LANES), fixed),
            pl.BlockSpec((1, ATT_VD), fixed),
        ],
        out_specs=pl.BlockSpec((1, tq, gw), lambda bi, gi, qi: (bi, qi, gi)),
        out_shape=jax.ShapeDtypeStruct((b, s, ATT_OUT), BF16),
        scratch_shapes=[
            pltpu.VMEM((streams, tq, tq), F32),
            pltpu.VMEM((streams, tq, tq), F32),
            pltpu.VMEM((streams, 1, tq), F32),
            pltpu.VMEM((streams, 1, tq), F32),
            pltpu.VMEM((streams, 1, tq), F32),
            pltpu.VMEM((streams, ATT_VD + ATT_DEN_ROWS, tq), F32),
        ],
        compiler_params=_params("parallel", "parallel", "parallel"),
        name="diff_attn",
    )(q, k, vt, lq1, lk1, lq2, lk2, lamc, sg)


def _out_proj_kernel(x_ref, u_ref, o_ref, w_ref, g_ref, y_ref):
    m = jnp.dot(u_ref[...], w_ref[0:CONV_CH, :], preferred_element_type=F32)
    m = m + jnp.dot(o_ref[...], w_ref[CONV_CH:, :], preferred_element_type=F32)
    y_ref[...] = x_ref[...] + _rms(m, g_ref[...])


def _out_proj(x, u, o, w_all, layer, g):
    n = x.shape[0]
    tm = TM_PROJ
    row = lambda i: (i, 0)
    return pl.pallas_call(
        _out_proj_kernel,
        grid=(n // tm,),
        in_specs=[
            pl.BlockSpec((tm, D_MODEL), row),
            pl.BlockSpec((tm, CONV_CH), row),
            pl.BlockSpec((tm, ATT_OUT), row),
            _layer_resident(layer, (CONV_CH + ATT_OUT, D_MODEL)),
            _resident((1, D_MODEL)),
        ],
        out_specs=pl.BlockSpec((tm, D_MODEL), row),
        out_shape=jax.ShapeDtypeStruct((n, D_MODEL), F32),
        compiler_params=_params("parallel"),
        name="out_proj",
    )(x, u, o, w_all, g)


def _ffn_kernel(tiles_per_seq, x_ref, xprev_ref, gpre_ref, wup_ref, cw_ref, cb_ref, wd_ref,
                gpost_ref, y_ref, xs_ref, xp_ref, h_ref, ua_ref, ub_ref, act_ref, acc_ref):
    tm = x_ref.shape[0]
    ng = tm // SUBLANES
    nf = D_FF // FC
    first = (pl.program_id(0) % tiles_per_seq) == 0
    nlc = D_MODEL // LANES
    pitch = ng + SUBLANES
    for c in range(nlc):
        for s in range(SUBLANES):
            xs_ref[c, s * pitch:s * pitch + ng, :] = x_ref[s * ng:(s + 1) * ng,
                                                           c * LANES:(c + 1) * LANES]

    def strided_rows(g):
        return jnp.concatenate(
            [xs_ref[c, pl.ds(g, SUBLANES, stride=pitch), :] for c in range(nlc)], axis=1)

    for k in range(ng // 2):
        xp = jnp.concatenate([strided_rows(2 * k), strided_rows(2 * k + 1)], axis=0)
        xp_ref[2 * SUBLANES * k:2 * SUBLANES * (k + 1), :] = xp
        h_ref[2 * SUBLANES * k:2 * SUBLANES * (k + 1), :] = _rms(xp, gpre_ref[...]).astype(BF16)
    hp = _rms(xprev_ref[...], gpre_ref[...])
    h_ref[tm:, :] = jnp.where(first, 0.0, hp).astype(BF16)
    acc_ref[...] = jnp.zeros(acc_ref.shape, F32)
    sub0 = lax.broadcasted_iota(jnp.int32, (SUBLANES, FC), 0) == 0

    def cols(chunk):
        return pl.ds(pl.multiple_of(chunk * FC, FC), FC)

    def up_proj(j, u_ref):
        h = h_ref[...]
        u_ref[0] = jnp.dot(h, wup_ref[:, cols(j)], preferred_element_type=F32)
        u_ref[1] = jnp.dot(h, wup_ref[:, cols(nf + j)], preferred_element_type=F32)

    def process(j, u_ref, buf, slot):
        def conv(half, chunk):
            halo = u_ref[half, tm + FFN_HALO - SUBLANES:tm + FFN_HALO, :]
            edge = {}
            for m in range(1, FFN_K):
                tail = u_ref[half, tm - SUBLANES * m:tm - SUBLANES * (m - 1), :]
                edge[m] = jnp.where(sub0, pltpu.roll(halo, m, 0), pltpu.roll(tail, 1, 0))

            def lag(d):
                if d == 0:
                    return u_ref[half, 0:tm, :]
                groups = [edge[m] for m in range(d, 0, -1)]
                return jnp.concatenate(groups + [u_ref[half, 0:tm - SUBLANES * d, :]], axis=0)

            out = cb_ref[:, cols(chunk)]
            for t in range(FFN_K):
                out = out + cw_ref[t:t + 1, cols(chunk)] * lag(FFN_K - 1 - t)
            return out

        g = conv(0, j)
        val = conv(1, nf + j)
        e = jnp.exp2(g * (GELU_K1 + GELU_K3 * (g * g)))
        act_ref[buf, :, slot * FC:(slot + 1) * FC] = (
            g * val * (1.0 / (1.0 + e))).astype(BF16)

    def down_pair(p, buf):
        rows = pl.ds(pl.multiple_of(p * 2 * FC, 2 * FC), 2 * FC)
        acc_ref[...] += jnp.dot(act_ref[buf], wd_ref[rows, :], preferred_element_type=F32)

    def pair(p, buf, with_down):
        j = 2 * p
        up_proj(j + 1, ub_ref)
        if with_down:
            down_pair(p - 1, 1 - buf)
        process(j, ua_ref, buf, 0)
        up_proj(j + 2, ua_ref)
        process(j + 1, ub_ref, buf, 1)

    up_proj(0, ua_ref)
    pair(0, 0, False)

    def body(q, carry):
        pair(2 * q + 1, 1, True)
        pair(2 * q + 2, 0, True)
        return carry

    npairs = (nf - 1) // 2
    assert npairs % 2 == 1
    lax.fori_loop(0, (npairs - 1) // 2, body, 0)
    down_pair(npairs - 1, 0)
    process(nf - 1, ua_ref, 1, 0)
    acc = acc_ref[...] + jnp.dot(act_ref[1, :, 0:FC], wd_ref[(nf - 1) * FC:nf * FC, :],
                                 preferred_element_type=F32)
    yp = xp_ref[...] + _rms(acc, gpost_ref[...])
    for g in range(ng):
        for c in range(nlc):
            xs_ref[c, pl.ds(g, SUBLANES, stride=pitch), :] = (
                yp[SUBLANES * g:SUBLANES * (g + 1), c * LANES:(c + 1) * LANES])
    for c in range(nlc):
        for s in range(SUBLANES):
            y_ref[s * ng:(s + 1) * ng, c * LANES:(c + 1) * LANES] = (
                xs_ref[c, s * pitch:s * pitch + ng, :])


def _ffn(x, gpre, wup_all, cw, cb, wd_all, layer, gpost, seq):
    n = x.shape[0]
    tm = TM_FFN
    nf = D_FF // FC
    assert nf % 2 == 1
    tiles_per_seq = seq // tm
    halo_per_tile = tm // FFN_HALO
    row = lambda i: (i, 0)
    prev = lambda i: (jnp.maximum(i * halo_per_tile - 1, 0), 0)
    return pl.pallas_call(
        functools.partial(_ffn_kernel, tiles_per_seq),
        grid=(n // tm,),
        in_specs=[
            pl.BlockSpec((tm, D_MODEL), row),
            pl.BlockSpec((FFN_HALO, D_MODEL), prev),
            _resident((1, D_MODEL)),
            _layer_resident(layer, (D_MODEL, 2 * D_FF)),
            _resident((FFN_K, 2 * D_FF)),
            _resident((1, 2 * D_FF)),
            _layer_resident(layer, (D_FF, D_MODEL)),
            _resident((1, D_MODEL)),
        ],
        out_specs=pl.BlockSpec((tm, D_MODEL), row),
        out_shape=jax.ShapeDtypeStruct((n, D_MODEL), F32),
        scratch_shapes=[
            pltpu.VMEM((D_MODEL // LANES, tm + SUBLANES * SUBLANES, LANES), F32),
            pltpu.VMEM((tm, D_MODEL), F32),
            pltpu.VMEM((FFN_HALO + tm, D_MODEL), BF16),
            pltpu.VMEM((2, FFN_HALO + tm, FC), F32),
            pltpu.VMEM((2, FFN_HALO + tm, FC), F32),
            pltpu.VMEM((2, tm, 2 * FC), BF16),
            pltpu.VMEM((tm, D_MODEL), F32),
        ],
        compiler_params=_params("parallel"),
        name="ffn",
    )(x, x, gpre, wup_all, cw, cb, wd_all, gpost)


def _rope_tables(seq):
    half = ROT_DIM // 2
    pos = jnp.arange(seq, dtype=F32)
    inv_freq = ROPE_THETA ** (-jnp.arange(0, ROT_DIM, 2, dtype=F32) / ROT_DIM)
    ang = pos[:, None] * inv_freq[None, :]
    cos, sin = jnp.cos(ang), jnp.sin(ang)
    ones = jnp.ones((seq, ATT_HD - ROT_DIM), F32)
    zeros_rest = jnp.zeros((seq, ATT_HD - ROT_DIM), F32)
    zeros_half = jnp.zeros((seq, half), F32)
    c = jnp.concatenate([cos, cos, ones], axis=1)
    a = jnp.concatenate([zeros_half, sin, zeros_rest], axis=1)
    b = jnp.concatenate([-sin, zeros_half, zeros_rest], axis=1)
    tile2 = lambda t: jnp.concatenate([t, t], axis=1)
    return tile2(c), tile2(a), tile2(b)


def kernel(x, pre_mix_norm, w_in, conv_w, conv_b, conv_ln_g, conv_ln_b, lambda_q1, lambda_k1,
           lambda_q2, lambda_k2, subln_g, w_out, post_mix_norm, pre_ffn_norm, w_up, ffn_conv_w,
           ffn_conv_b, w_down, post_ffn_norm):
    b, s, d = x.shape
    n = b * s
    rc, ra, rb = _rope_tables(s)
    xf = x.reshape(n, d)
    row = lambda t: t.reshape(1, -1)
    w_in, w_out, w_up, w_down = (w.astype(BF16) for w in (w_in, w_out, w_up, w_down))
    for l in range(DEPTH):
        lam_init = 0.8 - 0.6 * math.exp(-0.3 * l)
        lamc = jnp.zeros((1, LANES), F32).at[0, 0].set(lam_init).at[0, 1].set(1.0 - lam_init)
        u, q, k, vt = _in_proj(xf, row(pre_mix_norm[l]), w_in, l, rc, ra, rb,
                               conv_w[l], row(conv_b[l]), row(conv_ln_g[l]), row(conv_ln_b[l]), s)
        o = _attn(q.reshape(b, s, QK_COLS), k.reshape(b, s, QK_COLS), vt,
                  row(lambda_q1[l]), row(lambda_k1[l]), row(lambda_q2[l]), row(lambda_k2[l]),
                  lamc, row(subln_g[l]))
        xf = _out_proj(xf, u, o.reshape(n, ATT_OUT), w_out, l, row(post_mix_norm[l]))
        xf = _ffn(xf, row(pre_ffn_norm[l]), w_up, ffn_conv_w[l], row(ffn_conv_b[l]), w_down, l,
                  row(post_ffn_norm[l]), s)
    return xf.reshape(b, s, d)
```

```python
import functools
import math

import jax
import jax.numpy as jnp
from jax import lax
from jax.experimental import pallas as pl
from jax.experimental.pallas import tpu as pltpu

D_MODEL = 1024
DEPTH = 4
CONV_CH = 512
CONV_K = 31
ATT_HEADS = 4
ATT_HD = 64
ATT_VD = 2 * ATT_HD
ATT_OUT = ATT_HEADS * ATT_VD
QK_COLS = ATT_HEADS * 2 * ATT_HD
IN_COLS = 2 * CONV_CH + 2 * QK_COLS + ATT_OUT
ROT_DIM = ATT_HD // 4
ROPE_THETA = 500000.0
D_FF = 2816
FFN_K = 3
EPS = 1e-6

LANES = 128
SUBLANES = 8
VMEM_LIMIT = 56 * 1024 * 1024

TM_PROJ = 1024
CONV_HALO = 32
CONV_ROWS = 32
TQ = 512
ATT_GROUP = 4
ATT_DEN_ROWS = 16
TM_FFN = 512
FFN_HALO = 16
FC = 256
NEG_BIG = -1e30
GELU_K1 = -2.0 * math.log2(math.e) * math.sqrt(2.0 / math.pi)
GELU_K3 = GELU_K1 * 0.044715

F32 = jnp.float32
BF16 = jnp.bfloat16


def _params(*sem):
    return pltpu.CompilerParams(dimension_semantics=sem, vmem_limit_bytes=VMEM_LIMIT)


def _resident(shape):
    return pl.BlockSpec(shape, lambda *_: (0,) * len(shape), pipeline_mode=pl.Buffered(1))


def _layer_resident(layer, shape):
    return pl.BlockSpec((None,) + tuple(shape), lambda *_: (layer,) + (0,) * len(shape),
                        pipeline_mode=pl.Buffered(1))


def _rms(x, g):
    return x * lax.rsqrt(jnp.mean(x * x, axis=-1, keepdims=True) + EPS) * g


def _in_proj_kernel(tiles_per_seq, x_ref, xprev_ref, g_ref, w_ref, rc_ref, ra_ref, rb_ref,
                    cw_ref, cb_ref, lng_ref, lnb_ref, u_ref, q_ref, k_ref, vt_ref, sh_ref):
    first = (pl.program_id(0) % tiles_per_seq) == 0
    tm = x_ref.shape[0]
    h = _rms(x_ref[...], g_ref[...]).astype(BF16)
    hprev = _rms(xprev_ref[...], g_ref[...]).astype(BF16)
    hwin = jnp.concatenate([hprev, h], axis=0)

    def proj(lhs, c0):
        return jnp.dot(lhs, w_ref[:, c0:c0 + CONV_CH], preferred_element_type=F32)

    uwin = proj(hwin, 0) * jax.nn.sigmoid(proj(hwin, CONV_CH))
    sh_ref[0, 0:CONV_HALO, :] = jnp.where(first, 0.0, uwin[0:CONV_HALO])
    sh_ref[0, CONV_HALO:, :] = uwin[CONV_HALO:]
    win = sh_ref[0]
    nwin = CONV_HALO + tm
    for r in range(1, SUBLANES):
        sh_ref[r] = pltpu.roll(win, nwin - r, 0)
    base = CONV_HALO - (CONV_K - 1)
    groups = CONV_ROWS // SUBLANES

    def conv_rows(r0):
        acc = jnp.zeros((groups, SUBLANES, CONV_CH), F32)
        for k in range(CONV_K):
            phase = (base + k) % SUBLANES
            start = r0 + base + k - phase
            slab = sh_ref[phase, start:start + CONV_ROWS, :]
            acc = acc + cw_ref[k] * slab.reshape(groups, SUBLANES, CONV_CH)
        acc = acc.reshape(CONV_ROWS, CONV_CH) + cb_ref[...]
        mu = jnp.mean(acc, axis=-1, keepdims=True)
        d = acc - mu
        var = jnp.mean(d * d, axis=-1, keepdims=True)
        y = d * lax.rsqrt(var + EPS) * lng_ref[...] + lnb_ref[...]
        u_ref[r0:r0 + CONV_ROWS, :] = (y * jax.nn.sigmoid(y)).astype(BF16)

    rc, ra, rb = rc_ref[...], ra_ref[...], rb_ref[...]

    def rope(z):
        return (z * rc + pltpu.roll(z, ROT_DIM // 2, 1) * ra
                + pltpu.roll(z, LANES - ROT_DIM // 2, 1) * rb)

    for r0 in range(0, tm, CONV_ROWS):
        conv_rows(r0)

    zq = proj(h, 2 * CONV_CH)
    zk = proj(h, 2 * CONV_CH + QK_COLS)
    scale = math.log2(math.e) / math.sqrt(ATT_HD)
    for hd in range(ATT_HEADS):
        sl = slice(hd * LANES, (hd + 1) * LANES)
        q_ref[:, sl] = (rope(zq[:, sl]) * scale).astype(BF16)
        k_ref[:, sl] = rope(zk[:, sl]).astype(BF16)
    vt_ref[0] = proj(h, 2 * CONV_CH + 2 * QK_COLS).T.astype(BF16)


def _in_proj(x, g, w_all, layer, rc, ra, rb, cw, cb, lng, lnb, seq):
    n = x.shape[0]
    tm = TM_PROJ
    tiles_per_seq = seq // tm
    halo_per_tile = tm // CONV_HALO
    row = lambda i: (i, 0)
    prev = lambda i: (jnp.maximum(i * halo_per_tile - 1, 0), 0)
    pos = lambda i: (i % tiles_per_seq, 0)
    cw = jnp.broadcast_to(cw[:, None, :], (CONV_K, SUBLANES, CONV_CH))
    return pl.pallas_call(
        functools.partial(_in_proj_kernel, tiles_per_seq),
        grid=(n // tm,),
        in_specs=[
            pl.BlockSpec((tm, D_MODEL), row),
            pl.BlockSpec((CONV_HALO, D_MODEL), prev),
            _resident((1, D_MODEL)),
            _layer_resident(layer, (D_MODEL, IN_COLS)),
            pl.BlockSpec((tm, LANES), pos),
            pl.BlockSpec((tm, LANES), pos),
            pl.BlockSpec((tm, LANES), pos),
            _resident((CONV_K, SUBLANES, CONV_CH)),
            _resident((1, CONV_CH)),
            _resident((1, CONV_CH)),
            _resident((1, CONV_CH)),
        ],
        out_specs=[
            pl.BlockSpec((tm, CONV_CH), row),
            pl.BlockSpec((tm, QK_COLS), row),
            pl.BlockSpec((tm, QK_COLS), row),
            pl.BlockSpec((1, ATT_OUT, tm), lambda i: (i // tiles_per_seq, 0, i % tiles_per_seq)),
        ],
        out_shape=[
            jax.ShapeDtypeStruct((n, CONV_CH), BF16),
            jax.ShapeDtypeStruct((n, QK_COLS), BF16),
            jax.ShapeDtypeStruct((n, QK_COLS), BF16),
            jax.ShapeDtypeStruct((n // seq, ATT_OUT, seq), BF16),
        ],
        scratch_shapes=[pltpu.VMEM((SUBLANES, CONV_HALO + tm, CONV_CH), F32)],
        compiler_params=_params("parallel"),
        name="in_proj",
    )(x, x, g, w_all, rc, ra, rb, cw, cb, lng, lnb)


def _attn_kernel(q_ref, k_ref, vt_ref, lq1_ref, lk1_ref, lq2_ref, lk2_ref, lamc_ref, sg_ref,
                 o_ref, sa_ref, sb_ref, xa_ref, xb_ref, m_ref, acc_ref):
    qi = pl.program_id(2)
    tq = q_ref.shape[1]
    lane = lax.broadcasted_iota(jnp.int32, (tq, LANES), 1)
    qs = []
    for hh in range(ATT_GROUP):
        q = q_ref[0, :, hh * LANES:(hh + 1) * LANES]
        zero = jnp.zeros_like(q)
        qs += [jnp.where(lane < ATT_HD, q, zero), jnp.where(lane >= ATT_HD, q, zero)]

    m_ref[...] = jnp.full(m_ref.shape, NEG_BIG, F32)
    acc_ref[...] = jnp.zeros(acc_ref.shape, F32)
    ones_rows = jnp.ones((ATT_DEN_ROWS, tq), BF16)

    def scores(j, s_ref, x_ref):
        start = pl.multiple_of(j * tq, tq)
        for hh in range(ATT_GROUP):
            kb = k_ref[0, pl.ds(start, tq), hh * LANES:(hh + 1) * LANES]
            for n in (2 * hh, 2 * hh + 1):
                st = lax.dot_general(kb, qs[n], (((1,), (1,)), ((), ())),
                                     preferred_element_type=F32)
                s_ref[n] = st
                x_ref[n] = jnp.max(st, axis=0, keepdims=True)

    def softmax_pv(j, s_ref, x_ref):
        start = pl.multiple_of(j * tq, tq)
        for hh in range(ATT_GROUP):
            vtb = vt_ref[0, hh * ATT_VD:(hh + 1) * ATT_VD, pl.ds(start, tq)]
            vtb = jnp.concatenate([vtb, ones_rows], axis=0)
            for n in (2 * hh, 2 * hh + 1):
                m_prev = m_ref[n]
                m_new = jnp.maximum(m_prev, x_ref[n])
                alpha = jnp.exp2(m_prev - m_new)
                pt = jnp.exp2(s_ref[n] - m_new)
                acc_ref[n] = alpha * acc_ref[n] + jnp.dot(vtb, pt.astype(BF16),
                                                          preferred_element_type=F32)
                m_ref[n] = m_new

    def softmax_pv_diagonal(j, s_ref):
        start = pl.multiple_of(j * tq, tq)
        hq = tq // 2
        key = lax.broadcasted_iota(jnp.int32, (hq, hq), 0)
        qry = lax.broadcasted_iota(jnp.int32, (hq, hq), 1)
        tri = key <= qry
        lo, hi = slice(0, hq), slice(hq, tq)
        for hh in range(ATT_GROUP):
            vtb = vt_ref[0, hh * ATT_VD:(hh + 1) * ATT_VD, pl.ds(start, tq)]
            vtb = jnp.concatenate([vtb, ones_rows], axis=0)
            for n in (2 * hh, 2 * hh + 1):
                s_a = jnp.where(tri, s_ref[n, lo, lo], NEG_BIG)
                s_b = s_ref[n, lo, hi]
                s_c = jnp.where(tri, s_ref[n, hi, hi], NEG_BIG)
                max_lo = jnp.max(s_a, axis=0, keepdims=True)
                max_hi = jnp.maximum(jnp.max(s_b, axis=0, keepdims=True),
                                     jnp.max(s_c, axis=0, keepdims=True))
                m_prev = m_ref[n]
                m_new = jnp.maximum(m_prev, jnp.concatenate([max_lo, max_hi], axis=1))
                alpha = jnp.exp2(m_prev - m_new)
                p_a = jnp.exp2(s_a - m_new[:, lo]).astype(BF16)
                p_bc = jnp.exp2(jnp.concatenate([s_b, s_c], axis=0) - m_new[:, hi]).astype(BF16)
                acc_ref[n, :, lo] = alpha[:, lo] * acc_ref[n, :, lo] + jnp.dot(
                    vtb[:, lo], p_a, preferred_element_type=F32)
                acc_ref[n, :, hi] = alpha[:, hi] * acc_ref[n, :, hi] + jnp.dot(
                    vtb, p_bc, preferred_element_type=F32)
                m_ref[n] = m_new

    def by_parity(j, fn):
        lax.cond(j % 2 == 0, lambda: fn(sa_ref, xa_ref, sb_ref, xb_ref),
                 lambda: fn(sb_ref, xb_ref, sa_ref, xa_ref))

    scores(0, sa_ref, xa_ref)

    def body(j, carry):
        def run(s_cur, x_cur, s_nxt, x_nxt):
            scores(j + 1, s_nxt, x_nxt)
            softmax_pv(j, s_cur, x_cur)
        by_parity(j, run)
        return carry

    lax.fori_loop(0, qi, body, 0)
    by_parity(qi, lambda s_cur, x_cur, s_nxt, x_nxt: softmax_pv_diagonal(qi, s_cur))

    lam_init = lamc_ref[0:1, 0:1]
    out_scale = lamc_ref[0:1, 1:2]
    lam = (jnp.exp(jnp.sum(lq1_ref[...] * lk1_ref[...], axis=-1, keepdims=True))
           - jnp.exp(jnp.sum(lq2_ref[...] * lk2_ref[...], axis=-1, keepdims=True))
           + lam_init)

    def normalized(n, weight):
        inv_den = weight / acc_ref[n, ATT_VD:ATT_VD + 1, :]
        return acc_ref[n, 0:ATT_VD, :] * inv_den

    for hh in range(ATT_GROUP):
        ot = normalized(2 * hh, 1.0) - normalized(2 * hh + 1, lam)
        inv = lax.rsqrt(jnp.mean(ot * ot, axis=0, keepdims=True) + EPS) * out_scale
        o_ref[0, :, hh * ATT_VD:(hh + 1) * ATT_VD] = (
            (ot * inv).T * sg_ref[...]).astype(BF16)


def _attn(q, k, vt, lq1, lk1, lq2, lk2, lamc, sg):
    b, s, _ = q.shape
    tq = TQ
    gw = ATT_GROUP * LANES
    streams = 2 * ATT_GROUP
    fixed = lambda bi, gi, qi: (0, 0)
    vec = pl.BlockSpec((1, ATT_HD), fixed)
    return pl.pallas_call(
        _attn_kernel,
        grid=(b, ATT_HEADS // ATT_GROUP, s // tq),
        in_specs=[
            pl.BlockSpec((1, tq, gw), lambda bi, gi, qi: (bi, qi, gi)),
            pl.BlockSpec((1, s, gw), lambda bi, gi, qi: (bi, 0, gi)),
            pl.BlockSpec((1, gw, s), lambda bi, gi, qi: (bi, gi, 0)),
            vec, vec, vec, vec,
            pl.BlockSpec((1, LANES), fixed),
            pl.BlockSpec((1, ATT_VD), fixed),
        ],
        out_specs=pl.BlockSpec((1, tq, gw), lambda bi, gi, qi: (bi, qi, gi)),
        out_shape=jax.ShapeDtypeStruct((b, s, ATT_OUT), BF16),
        scratch_shapes=[
            pltpu.VMEM((streams, tq, tq), F32),
            pltpu.VMEM((streams, tq, tq), F32),
            pltpu.VMEM((streams, 1, tq), F32),
            pltpu.VMEM((streams, 1, tq), F32),
            pltpu.VMEM((streams, 1, tq), F32),
            pltpu.VMEM((streams, ATT_VD + ATT_DEN_ROWS, tq), F32),
        ],
        compiler_params=_params("parallel", "parallel", "parallel"),
        name="diff_attn",
    )(q, k, vt, lq1, lk1, lq2, lk2, lamc, sg)


def _out_proj_kernel(x_ref, u_ref, o_ref, w_ref, g_ref, y_ref):
    mixed = jnp.concatenate([u_ref[...], o_ref[...]], axis=1)
    m = jnp.dot(mixed, w_ref[...], preferred_element_type=F32)
    y_ref[...] = x_ref[...] + _rms(m, g_ref[...])


def _out_proj(x, u, o, w_all, layer, g):
    n = x.shape[0]
    tm = TM_PROJ
    row = lambda i: (i, 0)
    return pl.pallas_call(
        _out_proj_kernel,
        grid=(n // tm,),
        in_specs=[
            pl.BlockSpec((tm, D_MODEL), row),
            pl.BlockSpec((tm, CONV_CH), row),
            pl.BlockSpec((tm, ATT_OUT), row),
            _layer_resident(layer, (CONV_CH + ATT_OUT, D_MODEL)),
            _resident((1, D_MODEL)),
        ],
        out_specs=pl.BlockSpec((tm, D_MODEL), row),
        out_shape=jax.ShapeDtypeStruct((n, D_MODEL), F32),
        compiler_params=_params("parallel"),
        name="out_proj",
    )(x, u, o, w_all, g)


def _ffn_kernel(tiles_per_seq, x_ref, xprev_ref, gpre_ref, wup_ref, cw_ref, cb_ref, wd_ref,
                gpost_ref, y_ref, xs_ref, xp_ref, h_ref, ua_ref, ub_ref, act_ref, acc_ref):
    tm = x_ref.shape[0]
    ng = tm // SUBLANES
    nf = D_FF // FC
    first = (pl.program_id(0) % tiles_per_seq) == 0
    nlc = D_MODEL // LANES
    pitch = ng + SUBLANES
    for c in range(nlc):
        for s in range(SUBLANES):
            xs_ref[c, s * pitch:s * pitch + ng, :] = x_ref[s * ng:(s + 1) * ng,
                                                           c * LANES:(c + 1) * LANES]

    def strided_rows(g):
        return jnp.concatenate(
            [xs_ref[c, pl.ds(g, SUBLANES, stride=pitch), :] for c in range(nlc)], axis=1)

    for k in range(ng // 2):
        xp = jnp.concatenate([strided_rows(2 * k), strided_rows(2 * k + 1)], axis=0)
        xp_ref[2 * SUBLANES * k:2 * SUBLANES * (k + 1), :] = xp
        h_ref[2 * SUBLANES * k:2 * SUBLANES * (k + 1), :] = _rms(xp, gpre_ref[...]).astype(BF16)
    hp = _rms(xprev_ref[...], gpre_ref[...])
    h_ref[tm:, :] = jnp.where(first, 0.0, hp).astype(BF16)
    acc_ref[...] = jnp.zeros(acc_ref.shape, F32)
    sub0 = lax.broadcasted_iota(jnp.int32, (SUBLANES, FC), 0) == 0

    def cols(chunk):
        return pl.ds(pl.multiple_of(chunk * FC, FC), FC)

    def up_proj(j, u_ref):
        h = h_ref[...]
        u_ref[0] = jnp.dot(h, wup_ref[:, cols(j)], preferred_element_type=F32)
        u_ref[1] = jnp.dot(h, wup_ref[:, cols(nf + j)], preferred_element_type=F32)

    def process(j, u_ref, buf, slot):
        def conv(half, chunk):
            halo = u_ref[half, tm + FFN_HALO - SUBLANES:tm + FFN_HALO, :]
            edge = {}
            for m in range(1, FFN_K):
                tail = u_ref[half, tm - SUBLANES * m:tm - SUBLANES * (m - 1), :]
                edge[m] = jnp.where(sub0, pltpu.roll(halo, m, 0), pltpu.roll(tail, 1, 0))

            def lag(d):
                if d == 0:
                    return u_ref[half, 0:tm, :]
                groups = [edge[m] for m in range(d, 0, -1)]
                return jnp.concatenate(groups + [u_ref[half, 0:tm - SUBLANES * d, :]], axis=0)

            out = cb_ref[:, cols(chunk)]
            for t in range(FFN_K):
                out = out + cw_ref[t:t + 1, cols(chunk)] * lag(FFN_K - 1 - t)
            return out

        g = conv(0, j)
        val = conv(1, nf + j)
        e = jnp.exp2(g * (GELU_K1 + GELU_K3 * (g * g)))
        act_ref[buf, :, slot * FC:(slot + 1) * FC] = (
            g * val * (1.0 / (1.0 + e))).astype(BF16)

    def down_pair(p, buf):
        rows = pl.ds(pl.multiple_of(p * 2 * FC, 2 * FC), 2 * FC)
        acc_ref[...] += jnp.dot(act_ref[buf], wd_ref[rows, :], preferred_element_type=F32)

    def pair(p, buf, with_down):
        j = 2 * p
        up_proj(j + 1, ub_ref)
        if with_down:
            down_pair(p - 1, 1 - buf)
        process(j, ua_ref, buf, 0)
        up_proj(j + 2, ua_ref)
        process(j + 1, ub_ref, buf, 1)

    up_proj(0, ua_ref)
    pair(0, 0, False)

    def body(q, carry):
        pair(2 * q + 1, 1, True)
        pair(2 * q + 2, 0, True)
        return carry

    npairs = (nf - 1) // 2
    assert npairs % 2 == 1
    lax.fori_loop(0, (npairs - 1) // 2, body, 0)
    down_pair(npairs - 1, 0)
    process(nf - 1, ua_ref, 1, 0)
    acc = acc_ref[...] + jnp.dot(act_ref[1, :, 0:FC], wd_ref[(nf - 1) * FC:nf * FC, :],
                                 preferred_element_type=F32)
    yp = xp_ref[...] + _rms(acc, gpost_ref[...])
    for g in range(ng):
        for c in range(nlc):
            xs_ref[c, pl.ds(g, SUBLANES, stride=pitch), :] = (
                yp[SUBLANES * g:SUBLANES * (g + 1), c * LANES:(c + 1) * LANES])
    for c in range(nlc):
        for s in range(SUBLANES):
            y_ref[s * ng:(s + 1) * ng, c * LANES:(c + 1) * LANES] = (
                xs_ref[c, s * pitch:s * pitch + ng, :])


def _ffn(x, gpre, wup_all, cw, cb, wd_all, layer, gpost, seq):
    n = x.shape[0]
    tm = TM_FFN
    nf = D_FF // FC
    assert nf % 2 == 1
    tiles_per_seq = seq // tm
    halo_per_tile = tm // FFN_HALO
    row = lambda i: (i, 0)
    prev = lambda i: (jnp.maximum(i * halo_per_tile - 1, 0), 0)
    return pl.pallas_call(
        functools.partial(_ffn_kernel, tiles_per_seq),
        grid=(n // tm,),
        in_specs=[
            pl.BlockSpec((tm, D_MODEL), row),
            pl.BlockSpec((FFN_HALO, D_MODEL), prev),
            _resident((1, D_MODEL)),
            _layer_resident(layer, (D_MODEL, 2 * D_FF)),
            _resident((FFN_K, 2 * D_FF)),
            _resident((1, 2 * D_FF)),
            _layer_resident(layer, (D_FF, D_MODEL)),
            _resident((1, D_MODEL)),
        ],
        out_specs=pl.BlockSpec((tm, D_MODEL), row),
        out_shape=jax.ShapeDtypeStruct((n, D_MODEL), F32),
        scratch_shapes=[
            pltpu.VMEM((D_MODEL // LANES, tm + SUBLANES * SUBLANES, LANES), F32),
            pltpu.VMEM((tm, D_MODEL), F32),
            pltpu.VMEM((FFN_HALO + tm, D_MODEL), BF16),
            pltpu.VMEM((2, FFN_HALO + tm, FC), F32),
            pltpu.VMEM((2, FFN_HALO + tm, FC), F32),
            pltpu.VMEM((2, tm, 2 * FC), BF16),
            pltpu.VMEM((tm, D_MODEL), F32),
        ],
        compiler_params=_params("parallel"),
        name="ffn",
    )(x, x, gpre, wup_all, cw, cb, wd_all, gpost)


def _rope_tables(seq):
    half = ROT_DIM // 2
    pos = jnp.arange(seq, dtype=F32)
    inv_freq = ROPE_THETA ** (-jnp.arange(0, ROT_DIM, 2, dtype=F32) / ROT_DIM)
    ang = pos[:, None] * inv_freq[None, :]
    cos, sin = jnp.cos(ang), jnp.sin(ang)
    ones = jnp.ones((seq, ATT_HD - ROT_DIM), F32)
    zeros_rest = jnp.zeros((seq, ATT_HD - ROT_DIM), F32)
    zeros_half = jnp.zeros((seq, half), F32)
    c = jnp.concatenate([cos, cos, ones], axis=1)
    a = jnp.concatenate([zeros_half, sin, zeros_rest], axis=1)
    b = jnp.concatenate([-sin, zeros_half, zeros_rest], axis=1)
    tile2 = lambda t: jnp.concatenate([t, t], axis=1)
    return tile2(c), tile2(a), tile2(b)


def kernel(x, pre_mix_norm, w_in, conv_w, conv_b, conv_ln_g, conv_ln_b, lambda_q1, lambda_k1,
           lambda_q2, lambda_k2, subln_g, w_out, post_mix_norm, pre_ffn_norm, w_up, ffn_conv_w,
           ffn_conv_b, w_down, post_ffn_norm):
    b, s, d = x.shape
    n = b * s
    rc, ra, rb = _rope_tables(s)
    xf = x.reshape(n, d)
    row = lambda t: t.reshape(1, -1)
    w_in, w_out, w_up, w_down = (w.astype(BF16) for w in (w_in, w_out, w_up, w_down))
    for l in range(DEPTH):
        lam_init = 0.8 - 0.6 * math.exp(-0.3 * l)
        lamc = jnp.zeros((1, LANES), F32).at[0, 0].set(lam_init).at[0, 1].set(1.0 - lam_init)
        u, q, k, vt = _in_proj(xf, row(pre_mix_norm[l]), w_in, l, rc, ra, rb,
                               conv_w[l], row(conv_b[l]), row(conv_ln_g[l]), row(conv_ln_b[l]), s)
        o = _attn(q.reshape(b, s, QK_COLS), k.reshape(b, s, QK_COLS), vt,
                  row(lambda_q1[l]), row(lambda_k1[l]), row(lambda_q2[l]), row(lambda_k2[l]),
                  lamc, row(subln_g[l]))
        xf = _out_proj(xf, u, o.reshape(n, ATT_OUT), w_out, l, row(post_mix_norm[l]))
        xf = _ffn(xf, row(pre_ffn_norm[l]), w_up, ffn_conv_w[l], row(ffn_conv_b[l]), w_down, l,
                  row(post_ffn_norm[l]), s)
    return xf.reshape(b, s, d)
```

```python
import functools
import math

import jax
import jax.numpy as jnp
from jax import lax
from jax.experimental import pallas as pl
from jax.experimental.pallas import tpu as pltpu

D_MODEL = 1024
DEPTH = 4
CONV_CH = 512
CONV_K = 31
ATT_HEADS = 4
ATT_HD = 64
ATT_VD = 2 * ATT_HD
ATT_OUT = ATT_HEADS * ATT_VD
QK_COLS = ATT_HEADS * 2 * ATT_HD
IN_COLS = 2 * CONV_CH + 2 * QK_COLS + ATT_OUT
ROT_DIM = ATT_HD // 4
ROPE_THETA = 500000.0
D_FF = 2816
FFN_K = 3
EPS = 1e-6

LANES = 128
SUBLANES = 8
VMEM_LIMIT = 56 * 1024 * 1024

TM_PROJ = 1024
CONV_HALO = 32
CONV_ROWS = 32
TQ = 512
ATT_GROUP = 4
ATT_DEN_ROWS = 16
TM_FFN = 512
FFN_HALO = 16
FC = 256
NEG_BIG = -1e30
GELU_K1 = -2.0 * math.log2(math.e) * math.sqrt(2.0 / math.pi)
GELU_K3 = GELU_K1 * 0.044715

F32 = jnp.float32
BF16 = jnp.bfloat16


def _params(*sem):
    return pltpu.CompilerParams(dimension_semantics=sem, vmem_limit_bytes=VMEM_LIMIT)


def _resident(shape):
    return pl.BlockSpec(shape, lambda *_: (0,) * len(shape), pipeline_mode=pl.Buffered(1))


def _layer_resident(layer, shape):
    return pl.BlockSpec((None,) + tuple(shape), lambda *_: (layer,) + (0,) * len(shape),
                        pipeline_mode=pl.Buffered(1))


def _rms(x, g):
    return x * lax.rsqrt(jnp.mean(x * x, axis=-1, keepdims=True) + EPS) * g


def _in_proj_kernel(tiles_per_seq, x_ref, xprev_ref, g_ref, w_ref, rc_ref, ra_ref, rb_ref,
                    cw_ref, cb_ref, lng_ref, lnb_ref, u_ref, q_ref, k_ref, vt_ref, sh_ref):
    first = (pl.program_id(0) % tiles_per_seq) == 0
    tm = x_ref.shape[0]
    h = _rms(x_ref[...], g_ref[...]).astype(BF16)
    hprev = _rms(xprev_ref[...], g_ref[...]).astype(BF16)
    hwin = jnp.concatenate([hprev, h], axis=0)

    def proj(lhs, c0):
        return jnp.dot(lhs, w_ref[:, c0:c0 + CONV_CH], preferred_element_type=F32)

    uwin = proj(hwin, 0) * jax.nn.sigmoid(proj(hwin, CONV_CH))
    sh_ref[0, 0:CONV_HALO, :] = jnp.where(first, 0.0, uwin[0:CONV_HALO])
    sh_ref[0, CONV_HALO:, :] = uwin[CONV_HALO:]
    win = sh_ref[0]
    nwin = CONV_HALO + tm
    for r in range(1, SUBLANES):
        sh_ref[r] = pltpu.roll(win, nwin - r, 0)
    base = CONV_HALO - (CONV_K - 1)
    groups = CONV_ROWS // SUBLANES

    def conv_rows(r0):
        acc = jnp.zeros((groups, SUBLANES, CONV_CH), F32)
        for k in range(CONV_K):
            phase = (base + k) % SUBLANES
            start = r0 + base + k - phase
            slab = sh_ref[phase, start:start + CONV_ROWS, :]
            acc = acc + cw_ref[k] * slab.reshape(groups, SUBLANES, CONV_CH)
        acc = acc.reshape(CONV_ROWS, CONV_CH) + cb_ref[...]
        mu = jnp.mean(acc, axis=-1, keepdims=True)
        d = acc - mu
        var = jnp.mean(d * d, axis=-1, keepdims=True)
        y = d * lax.rsqrt(var + EPS) * lng_ref[...] + lnb_ref[...]
        u_ref[r0:r0 + CONV_ROWS, :] = (y * jax.nn.sigmoid(y)).astype(BF16)

    rc, ra, rb = rc_ref[...], ra_ref[...], rb_ref[...]

    def rope(z):
        return (z * rc + pltpu.roll(z, ROT_DIM // 2, 1) * ra
                + pltpu.roll(z, LANES - ROT_DIM // 2, 1) * rb)

    for r0 in range(0, tm, CONV_ROWS):
        conv_rows(r0)

    zq = proj(h, 2 * CONV_CH)
    zk = proj(h, 2 * CONV_CH + QK_COLS)
    scale = math.log2(math.e) / math.sqrt(ATT_HD)
    for hd in range(ATT_HEADS):
        sl = slice(hd * LANES, (hd + 1) * LANES)
        q_ref[:, sl] = (rope(zq[:, sl]) * scale).astype(BF16)
        k_ref[:, sl] = rope(zk[:, sl]).astype(BF16)
    vt_ref[0] = proj(h, 2 * CONV_CH + 2 * QK_COLS).T.astype(BF16)


def _in_proj(x, g, w_all, layer, rc, ra, rb, cw, cb, lng, lnb, seq):
    n = x.shape[0]
    tm = TM_PROJ
    tiles_per_seq = seq // tm
    halo_per_tile = tm // CONV_HALO
    row = lambda i: (i, 0)
    prev = lambda i: (jnp.maximum(i * halo_per_tile - 1, 0), 0)
    pos = lambda i: (i % tiles_per_seq, 0)
    cw = jnp.broadcast_to(cw[:, None, :], (CONV_K, SUBLANES, CONV_CH))
    return pl.pallas_call(
        functools.partial(_in_proj_kernel, tiles_per_seq),
        grid=(n // tm,),
        in_specs=[
            pl.BlockSpec((tm, D_MODEL), row),
            pl.BlockSpec((CONV_HALO, D_MODEL), prev),
            _resident((1, D_MODEL)),
            _layer_resident(layer, (D_MODEL, IN_COLS)),
            pl.BlockSpec((tm, LANES), pos),
            pl.BlockSpec((tm, LANES), pos),
            pl.BlockSpec((tm, LANES), pos),
            _resident((CONV_K, SUBLANES, CONV_CH)),
            _resident((1, CONV_CH)),
            _resident((1, CONV_CH)),
            _resident((1, CONV_CH)),
        ],
        out_specs=[
            pl.BlockSpec((tm, CONV_CH), row),
            pl.BlockSpec((tm, QK_COLS), row),
            pl.BlockSpec((tm, QK_COLS), row),
            pl.BlockSpec((1, ATT_OUT, tm), lambda i: (i // tiles_per_seq, 0, i % tiles_per_seq)),
        ],
        out_shape=[
            jax.ShapeDtypeStruct((n, CONV_CH), BF16),
            jax.ShapeDtypeStruct((n, QK_COLS), BF16),
            jax.ShapeDtypeStruct((n, QK_COLS), BF16),
            jax.ShapeDtypeStruct((n // seq, ATT_OUT, seq), BF16),
        ],
        scratch_shapes=[pltpu.VMEM((SUBLANES, CONV_HALO + tm, CONV_CH), F32)],
        compiler_params=_params("parallel"),
        name="in_proj",
    )(x, x, g, w_all, rc, ra, rb, cw, cb, lng, lnb)


def _attn_kernel(q_ref, k_ref, vt_ref, lq1_ref, lk1_ref, lq2_ref, lk2_ref, lamc_ref, sg_ref,
                 o_ref, sa_ref, sb_ref, xa_ref, xb_ref, m_ref, acc_ref):
    qi = pl.program_id(2)
    tq = q_ref.shape[1]
    lane = lax.broadcasted_iota(jnp.int32, (tq, LANES), 1)
    qs = []
    for hh in range(ATT_GROUP):
        q = q_ref[0, :, hh * LANES:(hh + 1) * LANES]
        zero = jnp.zeros_like(q)
        qs += [jnp.where(lane < ATT_HD, q, zero), jnp.where(lane >= ATT_HD, q, zero)]

    m_ref[...] = jnp.full(m_ref.shape, NEG_BIG, F32)
    acc_ref[...] = jnp.zeros(acc_ref.shape, F32)
    ones_rows = jnp.ones((ATT_DEN_ROWS, tq), BF16)

    def scores(j, s_ref, x_ref):
        start = pl.multiple_of(j * tq, tq)
        for hh in range(ATT_GROUP):
            kb = k_ref[0, pl.ds(start, tq), hh * LANES:(hh + 1) * LANES]
            for n in (2 * hh, 2 * hh + 1):
                st = lax.dot_general(kb, qs[n], (((1,), (1,)), ((), ())),
                                     preferred_element_type=F32)
                s_ref[n] = st
                x_ref[n] = jnp.max(st, axis=0, keepdims=True)

    def softmax_pv(j, s_ref, x_ref):
        start = pl.multiple_of(j * tq, tq)
        for hh in range(ATT_GROUP):
            vtb = vt_ref[0, hh * ATT_VD:(hh + 1) * ATT_VD, pl.ds(start, tq)]
            vtb = jnp.concatenate([vtb, ones_rows], axis=0)
            for n in (2 * hh, 2 * hh + 1):
                m_prev = m_ref[n]
                m_new = jnp.maximum(m_prev, x_ref[n])
                alpha = jnp.exp2(m_prev - m_new)
                pt = jnp.exp2(s_ref[n] - m_new)
                acc_ref[n] = alpha * acc_ref[n] + jnp.dot(vtb, pt.astype(BF16),
                                                          preferred_element_type=F32)
                m_ref[n] = m_new

    def softmax_pv_diagonal(j, s_ref):
        start = pl.multiple_of(j * tq, tq)
        hq = tq // 2
        key = lax.broadcasted_iota(jnp.int32, (hq, hq), 0)
        qry = lax.broadcasted_iota(jnp.int32, (hq, hq), 1)
        tri = key <= qry
        lo, hi = slice(0, hq), slice(hq, tq)
        for hh in range(ATT_GROUP):
            vtb = vt_ref[0, hh * ATT_VD:(hh + 1) * ATT_VD, pl.ds(start, tq)]
            vtb = jnp.concatenate([vtb, ones_rows], axis=0)
            for n in (2 * hh, 2 * hh + 1):
                s_a = jnp.where(tri, s_ref[n, lo, lo], NEG_BIG)
                s_b = s_ref[n, lo, hi]
                s_c = jnp.where(tri, s_ref[n, hi, hi], NEG_BIG)
                max_lo = jnp.max(s_a, axis=0, keepdims=True)
                max_hi = jnp.maximum(jnp.max(s_b, axis=0, keepdims=True),
                                     jnp.max(s_c, axis=0, keepdims=True))
                m_prev = m_ref[n]
                m_new = jnp.maximum(m_prev, jnp.concatenate([max_lo, max_hi], axis=1))
                alpha = jnp.exp2(m_prev - m_new)
                p_a = jnp.exp2(s_a - m_new[:, lo]).astype(BF16)
                p_bc = jnp.exp2(jnp.concatenate([s_b, s_c], axis=0) - m_new[:, hi]).astype(BF16)
                acc_ref[n, :, lo] = alpha[:, lo] * acc_ref[n, :, lo] + jnp.dot(
                    vtb[:, lo], p_a, preferred_element_type=F32)
                acc_ref[n, :, hi] = alpha[:, hi] * acc_ref[n, :, hi] + jnp.dot(
                    vtb, p_bc, preferred_element_type=F32)
                m_ref[n] = m_new

    def by_parity(j, fn):
        lax.cond(j % 2 == 0, lambda: fn(sa_ref, xa_ref, sb_ref, xb_ref),
                 lambda: fn(sb_ref, xb_ref, sa_ref, xa_ref))

    scores(0, sa_ref, xa_ref)

    def body(j, carry):
        def run(s_cur, x_cur, s_nxt, x_nxt):
            scores(j + 1, s_nxt, x_nxt)
            softmax_pv(j, s_cur, x_cur)
        by_parity(j, run)
        return carry

    lax.fori_loop(0, qi, body, 0)
    by_parity(qi, lambda s_cur, x_cur, s_nxt, x_nxt: softmax_pv_diagonal(qi, s_cur))

    lam_init = lamc_ref[0:1, 0:1]
    out_scale = lamc_ref[0:1, 1:2]
    lam = (jnp.exp(jnp.sum(lq1_ref[...] * lk1_ref[...], axis=-1, keepdims=True))
           - jnp.exp(jnp.sum(lq2_ref[...] * lk2_ref[...], axis=-1, keepdims=True))
           + lam_init)

    def normalized(n, weight):
        inv_den = weight / acc_ref[n, ATT_VD:ATT_VD + 1, :]
        return acc_ref[n, 0:ATT_VD, :] * inv_den

    for hh in range(ATT_GROUP):
        ot = normalized(2 * hh, 1.0) - normalized(2 * hh + 1, lam)
        inv = lax.rsqrt(jnp.mean(ot * ot, axis=0, keepdims=True) + EPS) * out_scale
        o_ref[0, :, hh * ATT_VD:(hh + 1) * ATT_VD] = (
            (ot * inv).T * sg_ref[...]).astype(BF16)


def _attn(q, k, vt, lq1, lk1, lq2, lk2, lamc, sg):
    b, s, _ = q.shape
    tq = TQ
    gw = ATT_GROUP * LANES
    streams = 2 * ATT_GROUP
    fixed = lambda bi, gi, qi: (0, 0)
    vec = pl.BlockSpec((1, ATT_HD), fixed)
    return pl.pallas_call(
        _attn_kernel,
        grid=(b, ATT_HEADS // ATT_GROUP, s // tq),
        in_specs=[
            pl.BlockSpec((1, tq, gw), lambda bi, gi, qi: (bi, qi, gi)),
            pl.BlockSpec((1, s, gw), lambda bi, gi, qi: (bi, 0, gi)),
            pl.BlockSpec((1, gw, s), lambda bi, gi, qi: (bi, gi, 0)),
            vec, vec, vec, vec,
            pl.BlockSpec((1, LANES), fixed),
            pl.BlockSpec((1, ATT_VD), fixed),
        ],
        out_specs=pl.BlockSpec((1, tq, gw), lambda bi, gi, qi: (bi, qi, gi)),
        out_shape=jax.ShapeDtypeStruct((b, s, ATT_OUT), BF16),
        scratch_shapes=[
            pltpu.VMEM((streams, tq, tq), F32),
            pltpu.VMEM((streams, tq, tq), F32),
            pltpu.VMEM((streams, 1, tq), F32),
            pltpu.VMEM((streams, 1, tq), F32),
            pltpu.VMEM((streams, 1, tq), F32),
            pltpu.VMEM((streams, ATT_VD + ATT_DEN_ROWS, tq), F32),
        ],
        compiler_params=_params("parallel", "parallel", "parallel"),
        name="diff_attn",
    )(q, k, vt, lq1, lk1, lq2, lk2, lamc, sg)


def _out_proj_kernel(x_ref, u_ref, o_ref, w_ref, g_ref, y_ref):
    m = jnp.dot(u_ref[...], w_ref[0:CONV_CH, :], preferred_element_type=F32)
    m = m + jnp.dot(o_ref[...], w_ref[CONV_CH:, :], preferred_element_type=F32)
    y_ref[...] = x_ref[...] + _rms(m, g_ref[...])


def _out_proj(x, u, o, w_all, layer, g):
    n = x.shape[0]
    tm = TM_PROJ
    row = lambda i: (i, 0)
    return pl.pallas_call(
        _out_proj_kernel,
        grid=(n // tm,),
        in_specs=[
            pl.BlockSpec((tm, D_MODEL), row),
            pl.BlockSpec((tm, CONV_CH), row),
            pl.BlockSpec((tm, ATT_OUT), row),
            _layer_resident(layer, (CONV_CH + ATT_OUT, D_MODEL)),
            _resident((1, D_MODEL)),
        ],
        out_specs=pl.BlockSpec((tm, D_MODEL), row),
        out_shape=jax.ShapeDtypeStruct((n, D_MODEL), F32),
        compiler_params=_params("parallel"),
        name="out_proj",
    )(x, u, o, w_all, g)


def _ffn_kernel(tiles_per_seq, x_ref, xprev_ref, u_ref, uprev_ref, o_ref, oprev_ref, wmix_ref,
                gmix_ref, gpre_ref, wup_ref, cw_ref, cb_ref, wd_ref,
                gpost_ref, y_ref, xs_ref, xp_ref, h_ref, ua_ref, ub_ref, act_ref, acc_ref):
    def mixed(xr, ur, orr):
        m = jnp.dot(ur[...], wmix_ref[0:CONV_CH, :], preferred_element_type=F32)
        m = m + jnp.dot(orr[...], wmix_ref[CONV_CH:, :], preferred_element_type=F32)
        return xr[...] + _rms(m, gmix_ref[...])

    x1 = mixed(x_ref, u_ref, o_ref)
    x1prev = mixed(xprev_ref, uprev_ref, oprev_ref)
    tm = x_ref.shape[0]
    ng = tm // SUBLANES
    nf = D_FF // FC
    first = (pl.program_id(0) % tiles_per_seq) == 0
    nlc = D_MODEL // LANES
    pitch = ng + SUBLANES
    for c in range(nlc):
        for s in range(SUBLANES):
            xs_ref[c, s * pitch:s * pitch + ng, :] = x1[s * ng:(s + 1) * ng,
                                                        c * LANES:(c + 1) * LANES]

    def strided_rows(g):
        return jnp.concatenate(
            [xs_ref[c, pl.ds(g, SUBLANES, stride=pitch), :] for c in range(nlc)], axis=1)

    for k in range(ng // 2):
        xp = jnp.concatenate([strided_rows(2 * k), strided_rows(2 * k + 1)], axis=0)
        xp_ref[2 * SUBLANES * k:2 * SUBLANES * (k + 1), :] = xp
        h_ref[2 * SUBLANES * k:2 * SUBLANES * (k + 1), :] = _rms(xp, gpre_ref[...]).astype(BF16)
    hp = _rms(x1prev, gpre_ref[...])
    h_ref[tm:, :] = jnp.where(first, 0.0, hp).astype(BF16)
    acc_ref[...] = jnp.zeros(acc_ref.shape, F32)
    sub0 = lax.broadcasted_iota(jnp.int32, (SUBLANES, FC), 0) == 0

    def cols(chunk):
        return pl.ds(pl.multiple_of(chunk * FC, FC), FC)

    def up_proj(j, u_ref):
        h = h_ref[...]
        u_ref[0] = jnp.dot(h, wup_ref[:, cols(j)], preferred_element_type=F32)
        u_ref[1] = jnp.dot(h, wup_ref[:, cols(nf + j)], preferred_element_type=F32)

    def process(j, u_ref, buf, slot):
        def conv(half, chunk):
            halo = u_ref[half, tm + FFN_HALO - SUBLANES:tm + FFN_HALO, :]
            edge = {}
            for m in range(1, FFN_K):
                tail = u_ref[half, tm - SUBLANES * m:tm - SUBLANES * (m - 1), :]
                edge[m] = jnp.where(sub0, pltpu.roll(halo, m, 0), pltpu.roll(tail, 1, 0))

            def lag(d):
                if d == 0:
                    return u_ref[half, 0:tm, :]
                groups = [edge[m] for m in range(d, 0, -1)]
                return jnp.concatenate(groups + [u_ref[half, 0:tm - SUBLANES * d, :]], axis=0)

            out = cb_ref[:, cols(chunk)]
            for t in range(FFN_K):
                out = out + cw_ref[t:t + 1, cols(chunk)] * lag(FFN_K - 1 - t)
            return out

        g = conv(0, j)
        val = conv(1, nf + j)
        e = jnp.exp2(g * (GELU_K1 + GELU_K3 * (g * g)))
        act_ref[buf, :, slot * FC:(slot + 1) * FC] = (
            g * val * (1.0 / (1.0 + e))).astype(BF16)

    def down_pair(p, buf):
        rows = pl.ds(pl.multiple_of(p * 2 * FC, 2 * FC), 2 * FC)
        acc_ref[...] += jnp.dot(act_ref[buf], wd_ref[rows, :], preferred_element_type=F32)

    def pair(p, buf, with_down):
        j = 2 * p
        up_proj(j + 1, ub_ref)
        if with_down:
            down_pair(p - 1, 1 - buf)
        process(j, ua_ref, buf, 0)
        up_proj(j + 2, ua_ref)
        process(j + 1, ub_ref, buf, 1)

    up_proj(0, ua_ref)
    pair(0, 0, False)

    def body(q, carry):
        pair(2 * q + 1, 1, True)
        pair(2 * q + 2, 0, True)
        return carry

    npairs = (nf - 1) // 2
    assert npairs % 2 == 1
    lax.fori_loop(0, (npairs - 1) // 2, body, 0)
    down_pair(npairs - 1, 0)
    process(nf - 1, ua_ref, 1, 0)
    acc = acc_ref[...] + jnp.dot(act_ref[1, :, 0:FC], wd_ref[(nf - 1) * FC:nf * FC, :],
                                 preferred_element_type=F32)
    yp = xp_ref[...] + _rms(acc, gpost_ref[...])
    for g in range(ng):
        for c in range(nlc):
            xs_ref[c, pl.ds(g, SUBLANES, stride=pitch), :] = (
                yp[SUBLANES * g:SUBLANES * (g + 1), c * LANES:(c + 1) * LANES])
    for c in range(nlc):
        for s in range(SUBLANES):
            y_ref[s * ng:(s + 1) * ng, c * LANES:(c + 1) * LANES] = (
                xs_ref[c, s * pitch:s * pitch + ng, :])


def _ffn(x, u, o, wmix_all, gmix, gpre, wup_all, cw, cb, wd_all, layer, gpost, seq):
    n = x.shape[0]
    tm = TM_FFN
    nf = D_FF // FC
    assert nf % 2 == 1
    tiles_per_seq = seq // tm
    halo_per_tile = tm // FFN_HALO
    row = lambda i: (i, 0)
    prev = lambda i: (jnp.maximum(i * halo_per_tile - 1, 0), 0)
    return pl.pallas_call(
        functools.partial(_ffn_kernel, tiles_per_seq),
        grid=(n // tm,),
        in_specs=[
            pl.BlockSpec((tm, D_MODEL), row),
            pl.BlockSpec((FFN_HALO, D_MODEL), prev),
            pl.BlockSpec((tm, CONV_CH), row),
            pl.BlockSpec((FFN_HALO, CONV_CH), prev),
            pl.BlockSpec((tm, ATT_OUT), row),
            pl.BlockSpec((FFN_HALO, ATT_OUT), prev),
            _layer_resident(layer, (CONV_CH + ATT_OUT, D_MODEL)),
            _resident((1, D_MODEL)),
            _resident((1, D_MODEL)),
            _layer_resident(layer, (D_MODEL, 2 * D_FF)),
            _resident((FFN_K, 2 * D_FF)),
            _resident((1, 2 * D_FF)),
            _layer_resident(layer, (D_FF, D_MODEL)),
            _resident((1, D_MODEL)),
        ],
        out_specs=pl.BlockSpec((tm, D_MODEL), row),
        out_shape=jax.ShapeDtypeStruct((n, D_MODEL), F32),
        scratch_shapes=[
            pltpu.VMEM((D_MODEL // LANES, tm + SUBLANES * SUBLANES, LANES), F32),
            pltpu.VMEM((tm, D_MODEL), F32),
            pltpu.VMEM((FFN_HALO + tm, D_MODEL), BF16),
            pltpu.VMEM((2, FFN_HALO + tm, FC), F32),
            pltpu.VMEM((2, FFN_HALO + tm, FC), F32),
            pltpu.VMEM((2, tm, 2 * FC), BF16),
            pltpu.VMEM((tm, D_MODEL), F32),
        ],
        compiler_params=_params("parallel"),
        name="ffn",
    )(x, x, u, u, o, o, wmix_all, gmix, gpre, wup_all, cw, cb, wd_all, gpost)


def _rope_tables(seq):
    half = ROT_DIM // 2
    pos = jnp.arange(seq, dtype=F32)
    inv_freq = ROPE_THETA ** (-jnp.arange(0, ROT_DIM, 2, dtype=F32) / ROT_DIM)
    ang = pos[:, None] * inv_freq[None, :]
    cos, sin = jnp.cos(ang), jnp.sin(ang)
    ones = jnp.ones((seq, ATT_HD - ROT_DIM), F32)
    zeros_rest = jnp.zeros((seq, ATT_HD - ROT_DIM), F32)
    zeros_half = jnp.zeros((seq, half), F32)
    c = jnp.concatenate([cos, cos, ones], axis=1)
    a = jnp.concatenate([zeros_half, sin, zeros_rest], axis=1)
    b = jnp.concatenate([-sin, zeros_half, zeros_rest], axis=1)
    tile2 = lambda t: jnp.concatenate([t, t], axis=1)
    return tile2(c), tile2(a), tile2(b)


def kernel(x, pre_mix_norm, w_in, conv_w, conv_b, conv_ln_g, conv_ln_b, lambda_q1, lambda_k1,
           lambda_q2, lambda_k2, subln_g, w_out, post_mix_norm, pre_ffn_norm, w_up, ffn_conv_w,
           ffn_conv_b, w_down, post_ffn_norm):
    b, s, d = x.shape
    n = b * s
    rc, ra, rb = _rope_tables(s)
    xf = x.reshape(n, d)
    row = lambda t: t.reshape(1, -1)
    w_in, w_out, w_up, w_down = (w.astype(BF16) for w in (w_in, w_out, w_up, w_down))
    for l in range(DEPTH):
        lam_init = 0.8 - 0.6 * math.exp(-0.3 * l)
        lamc = jnp.zeros((1, LANES), F32).at[0, 0].set(lam_init).at[0, 1].set(1.0 - lam_init)
        u, q, k, vt = _in_proj(xf, row(pre_mix_norm[l]), w_in, l, rc, ra, rb,
                               conv_w[l], row(conv_b[l]), row(conv_ln_g[l]), row(conv_ln_b[l]), s)
        o = _attn(q.reshape(b, s, QK_COLS), k.reshape(b, s, QK_COLS), vt,
                  row(lambda_q1[l]), row(lambda_k1[l]), row(lambda_q2[l]), row(lambda_k2[l]),
                  lamc, row(subln_g[l]))
        xf = _ffn(xf, u, o.reshape(n, ATT_OUT), w_out, row(post_mix_norm[l]),
                  row(pre_ffn_norm[l]), w_up, ffn_conv_w[l], row(ffn_conv_b[l]), w_down, l,
                  row(post_ffn_norm[l]), s)
    return xf.reshape(b, s, d)
```
